```python
import math
import jax, jax.numpy as jnp
from jax import lax
import numpy as np

D_MODEL = 2048
BATCH = 4
SEQ = 4096
DEPTH = 1
DEC_BATCH = 4
DEC_SEQ = 2048
PAST_LEN = 128

SSM_WIDTH = D_MODEL // 2
SSM_GROUP = 16
SSM_GROUPS = SSM_WIDTH // SSM_GROUP
SSM_STATE = 64
LOG_STEP_MIN = math.log(1e-3)
LOG_STEP_MAX = math.log(1e-1)
CONV_DIM = D_MODEL // 2
CONV_WIDTH = 31
CONV_PAD = CONV_WIDTH // 2
IN_COLS = SSM_WIDTH + 2 * CONV_DIM
PEER_HEADS = 8
PEER_QDIM = 256
PEER_HALF = PEER_QDIM // 2
N_KEYS = 128
N_EXPERTS = N_KEYS * N_KEYS
PEER_TOPK = 16
PEER_BLOCK = 128
PLE_DIM = 256
EPS = 1e-6

kernel_name = "hybrid_s5_conformer_peer_encoder"


def _rmsnorm(x, g):
    xf = x.astype(jnp.float32)
    y = xf * lax.rsqrt(jnp.mean(xf * xf, axis=-1, keepdims=True) + EPS) * g.astype(jnp.float32)
    return y.astype(x.dtype)


def _ssm_combine(left, right):
    a_i, b_i = left
    a_j, b_j = right
    return a_j * a_i, a_j * b_i + b_j


def _s5_branch(u, lam_re, lam_im, b_re, b_im, c_re, c_im, log_step, d_skip, w_glu, b_glu):
    dt = u.dtype
    bsz, seqlen, _ = u.shape
    f32 = jnp.float32
    uf = u.astype(f32).reshape(bsz, seqlen, SSM_GROUPS, SSM_GROUP)
    lam = lax.complex(lam_re.astype(f32), lam_im.astype(f32))
    step = jnp.exp(log_step.astype(f32))[..., None]
    lam_bar = jnp.exp(lam * step)
    bmat = lax.complex(b_re.astype(f32), b_im.astype(f32))
    b_bar = ((lam_bar - 1.0) / lam)[..., None] * bmat
    cmat = lax.complex(c_re.astype(f32), c_im.astype(f32))
    y = uf * d_skip.astype(f32).reshape(SSM_GROUPS, SSM_GROUP)
    for direction in range(2):
        bu = jnp.einsum('blgh,gph->blgp', uf, b_bar[direction])
        a = jnp.broadcast_to(lam_bar[direction], bu.shape)
        _, states = lax.associative_scan(_ssm_combine, (a, bu), reverse=(direction == 1), axis=1)
        y = y + jnp.real(jnp.einsum('ghp,blgp->blgh', cmat[direction], states))
    hg = jax.nn.gelu(y.reshape(bsz, seqlen, SSM_WIDTH), approximate=False).astype(dt)
    gate = jax.nn.sigmoid((hg @ w_glu + b_glu).astype(f32))
    return (hg.astype(f32) * gate).astype(dt)


def _conv_branch(vg, conv_w, conv_b, ln_g, ln_b):
    dt = vg.dtype
    f32 = jnp.float32
    v = vg[..., :CONV_DIM]
    g = vg[..., CONV_DIM:]
    h = (v.astype(f32) * jax.nn.sigmoid(g.astype(f32))).astype(dt)
    h = lax.conv_general_dilated(h, conv_w[:, None, :].astype(dt), window_strides=(1,),
                                 padding=[(CONV_PAD, CONV_PAD)],
                                 dimension_numbers=('NWC', 'WIO', 'NWC'),
                                 feature_group_count=CONV_DIM)
    hf = h.astype(f32) + conv_b.astype(f32)
    mu = jnp.mean(hf, axis=-1, keepdims=True)
    var = jnp.mean(jnp.square(hf - mu), axis=-1, keepdims=True)
    hn = (hf - mu) * lax.rsqrt(var + EPS) * ln_g.astype(f32) + ln_b.astype(f32)
    return jax.nn.silu(hn).astype(dt)


def _peer(xn, w_q, keys, u_tab, v_tab):
    f32 = jnp.float32
    bsz, seqlen, _ = xn.shape
    q = (xn @ w_q).reshape(bsz, seqlen, PEER_HEADS, 2, PEER_HALF)
    s = jnp.einsum('blhcd,hckd->blhck', q, keys).astype(f32)
    s_top, i_top = lax.top_k(s, PEER_TOPK)
    cand_s = (s_top[..., 0, :, None] + s_top[..., 1, None, :]).reshape(bsz, seqlen, PEER_HEADS, PEER_TOPK * PEER_TOPK)
    cand_i = (i_top[..., 0, :, None] * N_KEYS + i_top[..., 1, None, :]).reshape(bsz, seqlen, PEER_HEADS, PEER_TOPK * PEER_TOPK)
    best_s, best_pos = lax.top_k(cand_s, PEER_TOPK)
    e_idx = jnp.take_along_axis(cand_i, best_pos, axis=-1)
    gates = jax.nn.softmax(best_s, axis=-1)
    n_tok = bsz * seqlen
    n_blk = n_tok // PEER_BLOCK
    hk = PEER_HEADS * PEER_TOPK
    xb = xn.reshape(n_blk, PEER_BLOCK, D_MODEL)
    ib = e_idx.reshape(n_blk, PEER_BLOCK, hk)
    gb = gates.reshape(n_blk, PEER_BLOCK, hk)

    def expert_block(args):
        xc, ic, gc = args
        u_sel = jnp.take(u_tab, ic, axis=0)
        v_sel = jnp.take(v_tab, ic, axis=0)
        act = jax.nn.gelu(jnp.einsum('ced,cd->ce', u_sel, xc).astype(f32), approximate=False) * gc
        return jnp.einsum('ce,ced->cd', act.astype(xc.dtype), v_sel)

    out = lax.map(expert_block, (xb, ib, gb))
    return out.reshape(bsz, seqlen, D_MODEL)


def _layer(x, pe, norm_mix_g, w_in, w_gate, b_gate, lam_re, lam_im, b_re, b_im, c_re, c_im,
           log_step, d_skip, w_glu, b_glu, w_a_out, conv_w, conv_b, ln_g, ln_b, w_b_out, w_o,
           norm_ffn_g, w_q, keys, u_tab, v_tab, norm_ple_g, ple_w, ple_w_gate, ple_b_gate):
    dt = x.dtype
    f32 = jnp.float32
    h = _rmsnorm(x, norm_mix_g)
    z = h @ w_in
    a = _s5_branch(z[..., :SSM_WIDTH], lam_re, lam_im, b_re, b_im, c_re, c_im, log_step,
                   d_skip, w_glu, b_glu) @ w_a_out
    b = _conv_branch(z[..., SSM_WIDTH:], conv_w, conv_b, ln_g, ln_b) @ w_b_out
    g = jax.nn.sigmoid((h @ w_gate + b_gate).astype(f32))
    merged = (g[..., :D_MODEL] * a.astype(f32) + g[..., D_MODEL:] * b.astype(f32)).astype(dt)
    x = x + merged @ w_o
    x = x + _peer(_rmsnorm(x, norm_ffn_g), w_q, keys, u_tab, v_tab)
    pg = jax.nn.sigmoid((_rmsnorm(x, norm_ple_g) @ ple_w_gate + ple_b_gate).astype(f32))
    x = x + ((pe.astype(dt) @ ple_w).astype(f32) * pg).astype(dt)
    return x


def _trunk(x, p, layer_params, norm_final_g):
    for i in range(DEPTH):
        x = _layer(x, p[i], *[w[i] for w in layer_params])
    return _rmsnorm(x, norm_final_g)


def setup_inputs(seed: int = 0) -> dict:
    key = jax.random.key(seed)
    ks = jax.random.split(key, 40)
    f32 = jnp.float32

    def nrm(k, shape, scale):
        return jax.random.normal(k, shape, f32) * scale

    L = DEPTH
    G, P, HG = SSM_GROUPS, SSM_STATE, SSM_GROUP
    return {
        "x_prompt": nrm(ks[0], (BATCH, SEQ, D_MODEL), 1.0),
        "x_sample": nrm(ks[1], (DEC_BATCH, DEC_SEQ, D_MODEL), 1.0),
        "p_prompt": nrm(ks[2], (DEPTH, BATCH, SEQ, PLE_DIM), 1.0),
        "p_sample": nrm(ks[3], (DEPTH, DEC_BATCH, DEC_SEQ, PLE_DIM), 1.0),
        "norm_mix_g": 1.0 + nrm(ks[4], (L, D_MODEL), 0.02),
        "w_in": nrm(ks[5], (L, D_MODEL, IN_COLS), D_MODEL ** -0.5),
        "w_gate": nrm(ks[6], (L, D_MODEL, 2 * D_MODEL), D_MODEL ** -0.5),
        "b_gate": nrm(ks[7], (L, 2 * D_MODEL), 0.02),
        "ssm_lam_re": -0.5 + nrm(ks[8], (L, 2, G, P), 0.01),
        "ssm_lam_im": jnp.pi * jnp.arange(P, dtype=f32) + nrm(ks[9], (L, 2, G, P), 0.01),
        "ssm_b_re": nrm(ks[10], (L, 2, G, P, HG), (2 * HG) ** -0.5),
        "ssm_b_im": nrm(ks[11], (L, 2, G, P, HG), (2 * HG) ** -0.5),
        "ssm_c_re": nrm(ks[12], (L, 2, G, HG, P), (2 * P) ** -0.5),
        "ssm_c_im": nrm(ks[13], (L, 2, G, HG, P), (2 * P) ** -0.5),
        "ssm_log_step": jax.random.uniform(ks[14], (L, 2, G), f32, LOG_STEP_MIN, LOG_STEP_MAX),
        "ssm_d": nrm(ks[15], (L, SSM_WIDTH), 1.0),
        "ssm_w_glu": nrm(ks[16], (L, SSM_WIDTH, SSM_WIDTH), SSM_WIDTH ** -0.5),
        "ssm_b_glu": nrm(ks[17], (L, SSM_WIDTH), 0.02),
        "ssm_w_out": nrm(ks[18], (L, SSM_WIDTH, D_MODEL), SSM_WIDTH ** -0.5),
        "conv_w": nrm(ks[19], (L, CONV_WIDTH, CONV_DIM), CONV_WIDTH ** -0.5),
        "conv_b": nrm(ks[20], (L, CONV_DIM), 0.02),
        "conv_ln_g": 1.0 + nrm(ks[21], (L, CONV_DIM), 0.02),
        "conv_ln_b": nrm(ks[22], (L, CONV_DIM), 0.02),
        "conv_w_out": nrm(ks[23], (L, CONV_DIM, D_MODEL), CONV_DIM ** -0.5),
        "w_o": nrm(ks[24], (L, D_MODEL, D_MODEL), D_MODEL ** -0.5),
        "norm_ffn_g": 1.0 + nrm(ks[25], (L, D_MODEL), 0.02),
        "peer_w_q": nrm(ks[26], (L, D_MODEL, PEER_HEADS * PEER_QDIM), D_MODEL ** -0.5),
        "peer_keys": nrm(ks[27], (L, PEER_HEADS, 2, N_KEYS, PEER_HALF), PEER_HALF ** -0.5),
        "peer_u": nrm(ks[28], (L, N_EXPERTS, D_MODEL), D_MODEL ** -0.5),
        "peer_v": nrm(ks[29], (L, N_EXPERTS, D_MODEL), PEER_HEADS ** -0.5),
        "norm_ple_g": 1.0 + nrm(ks[30], (L, D_MODEL), 0.02),
        "ple_w": nrm(ks[31], (L, PLE_DIM, D_MODEL), PLE_DIM ** -0.5),
        "ple_w_gate": nrm(ks[32], (L, D_MODEL, D_MODEL), D_MODEL ** -0.5),
        "ple_b_gate": nrm(ks[33], (L, D_MODEL), 0.02),
        "norm_final_g": 1.0 + nrm(ks[34], (D_MODEL,), 0.02),
    }


def reference(x_prompt, x_sample, p_prompt, p_sample, norm_mix_g, w_in, w_gate, b_gate,
              ssm_lam_re, ssm_lam_im, ssm_b_re, ssm_b_im, ssm_c_re, ssm_c_im, ssm_log_step,
              ssm_d, ssm_w_glu, ssm_b_glu, ssm_w_out, conv_w, conv_b, conv_ln_g, conv_ln_b,
              conv_w_out, w_o, norm_ffn_g, peer_w_q, peer_keys, peer_u, peer_v, norm_ple_g,
              ple_w, ple_w_gate, ple_b_gate, norm_final_g):
    layer_params = (norm_mix_g, w_in, w_gate, b_gate, ssm_lam_re, ssm_lam_im, ssm_b_re, ssm_b_im,
                    ssm_c_re, ssm_c_im, ssm_log_step, ssm_d, ssm_w_glu, ssm_b_glu, ssm_w_out,
                    conv_w, conv_b, conv_ln_g, conv_ln_b, conv_w_out, w_o, norm_ffn_g, peer_w_q,
                    peer_keys, peer_u, peer_v, norm_ple_g, ple_w, ple_w_gate, ple_b_gate)
    y_prompt = _trunk(x_prompt, p_prompt, layer_params, norm_final_g)
    y_sample = _trunk(x_sample, p_sample, layer_params, norm_final_g)
    return (y_prompt, y_sample)
```

```python
import functools
import math

import jax
import jax.numpy as jnp
from jax import lax
from jax.experimental import pallas as pl
from jax.experimental.pallas import tpu as pltpu

F32 = jnp.float32
BF16 = jnp.bfloat16
I32 = jnp.int32

EPS = 1e-6
LANES = 128
SSM_GROUP = 16
SSM_STATE = 64
S5_CHUNK = 16
CONV_WIDTH = 31
CONV_PAD = CONV_WIDTH // 2
HALO = 16
PEER_HEADS = 8
PEER_HALF = 128
N_KEYS = 128
PEER_TOPK = 16
HK = PEER_HEADS * PEER_TOPK
CAND_PAIRS = tuple((i, j) for i in range(PEER_TOPK) for j in range(PEER_TOPK // (i + 1)))
N_CAND = -(-len(CAND_PAIRS) // 8) * 8
VMEM_LIMIT = 56 * 1024 * 1024


def _cparams(*sem):
    return pltpu.CompilerParams(dimension_semantics=sem, vmem_limit_bytes=VMEM_LIMIT)


def _tile(n, pref):
    t = min(n, pref)
    while n % t:
        t //= 2
    return t


def _sigmoid(x):
    return 1.0 / (1.0 + jnp.exp(-x))


def _gelu(x):
    return 0.5 * x * (1.0 + lax.erf(x * (1.0 / math.sqrt(2.0))))


def _rms(x, g):
    return x * lax.rsqrt(jnp.mean(x * x, axis=-1, keepdims=True) + EPS) * g


def _resident(shape):
    nd = len(shape)
    return pl.BlockSpec(shape, lambda *_: (0,) * nd, pipeline_mode=pl.Buffered(1))


def _norm_matmul_kernel(x_ref, g_ref, w_ref, b_ref, o_ref, h_ref, *, act):
    @pl.when(pl.program_id(1) == 0)
    def _():
        h_ref[...] = _rms(x_ref[...], g_ref[...]).astype(BF16)

    y = jnp.dot(h_ref[...], w_ref[...], preferred_element_type=F32) + b_ref[...]
    if act == "sigmoid":
        y = _sigmoid(y)
    o_ref[...] = y.astype(o_ref.dtype)


def _norm_matmul(x, g, w, b, act, out_dtype):
    n, d = x.shape
    cols = w.shape[1]
    tm = _tile(n, 512)
    tn = _tile(cols, 1024)
    return pl.pallas_call(
        functools.partial(_norm_matmul_kernel, act=act),
        grid=(n // tm, cols // tn),
        in_specs=[
            pl.BlockSpec((tm, d), lambda i, j: (i, 0)),
            pl.BlockSpec((1, d), lambda i, j: (0, 0)),
            pl.BlockSpec((d, tn), lambda i, j: (0, j)),
            pl.BlockSpec((1, tn), lambda i, j: (0, j)),
        ],
        out_specs=pl.BlockSpec((tm, tn), lambda i, j: (i, j)),
        out_shape=jax.ShapeDtypeStruct((n, cols), out_dtype),
        scratch_shapes=[pltpu.VMEM((tm, d), BF16)],
        compiler_params=_cparams("parallel", "arbitrary"),
        name="norm_matmul",
    )(x, g.reshape(1, d), w, b.reshape(1, cols))


def _s5_weights(lam_re, lam_im, b_re, b_im, c_re, c_im, log_step, d_skip):
    t = S5_CHUNK
    g = lam_re.shape[1]
    step = jnp.exp(log_step)[..., None]
    zr, zi = lam_re * step, lam_im * step
    dd = jnp.arange(t + 1, dtype=F32).reshape(t + 1, 1, 1, 1)
    mag = jnp.exp(dd * zr)
    pw_r, pw_i = mag * jnp.cos(dd * zi), mag * jnp.sin(dd * zi)
    nr, ni = pw_r[1] - 1.0, pw_i[1]
    den = lam_re * lam_re + lam_im * lam_im
    qr = (nr * lam_re + ni * lam_im) / den
    qi = (ni * lam_re - nr * lam_im) / den
    bb_r = qr[..., None] * b_re - qi[..., None] * b_im
    bb_i = qr[..., None] * b_im + qi[..., None] * b_re
    lb_r = pw_r[:t, ..., None] * bb_r - pw_i[:t, ..., None] * bb_i
    lb_i = pw_r[:t, ..., None] * bb_i + pw_i[:t, ..., None] * bb_r
    hi = lax.Precision.HIGHEST
    kern = (jnp.einsum("xgkp,dxgph->xgdkh", c_re, lb_r, precision=hi)
            - jnp.einsum("xgkp,dxgph->xgdkh", c_im, lb_i, precision=hi))
    tau = jnp.arange(t)[:, None]
    tt = jnp.arange(t)[None, :]
    mf = kern[0][:, jnp.clip(tt - tau, 0, t - 1)] * (tt >= tau)[None, :, :, None, None]
    mb = kern[1][:, jnp.clip(tau - tt, 0, t - 1)] * (tau >= tt)[None, :, :, None, None]
    m = jnp.transpose(mf + mb, (0, 1, 4, 2, 3))
    skip = (d_skip.reshape(g, 1, SSM_GROUP, 1, 1)
            * jnp.eye(t, dtype=F32).reshape(1, t, 1, t, 1)
            * jnp.eye(SSM_GROUP, dtype=F32).reshape(1, 1, SSM_GROUP, 1, SSM_GROUP))
    th = t * SSM_GROUP
    m = (m + skip).reshape(g, th, th)
    ef_r = jnp.transpose(lb_r[::-1, 0], (1, 0, 3, 2)).reshape(g, th, SSM_STATE)
    ef_i = jnp.transpose(lb_i[::-1, 0], (1, 0, 3, 2)).reshape(g, th, SSM_STATE)
    eb_r = jnp.transpose(lb_r[:, 1], (1, 0, 3, 2)).reshape(g, th, SSM_STATE)
    eb_i = jnp.transpose(lb_i[:, 1], (1, 0, 3, 2)).reshape(g, th, SSM_STATE)
    pf_r, pf_i = pw_r[1:, 0], pw_i[1:, 0]
    pb_r, pb_i = pw_r[1:, 1][::-1], pw_i[1:, 1][::-1]

    def carry_rows(cr, ci, pr, pi):
        re = cr[None] * pr[:, :, None, :] - ci[None] * pi[:, :, None, :]
        im = cr[None] * pi[:, :, None, :] + ci[None] * pr[:, :, None, :]
        re = jnp.transpose(re, (1, 3, 0, 2)).reshape(g, SSM_STATE, th)
        im = jnp.transpose(im, (1, 3, 0, 2)).reshape(g, SSM_STATE, th)
        return re, -im

    cf_r, cf_i = carry_rows(c_re[0], c_im[0], pf_r, pf_i)
    cb_r, cb_i = carry_rows(c_re[1], c_im[1], pb_r, pb_i)

    eye2 = jnp.eye(2, dtype=F32)

    def pair_diag(a):
        r, c = a.shape[1:]
        a = a.reshape(g // 2, 2, r, c)
        return jnp.einsum("parc,ab->parbc", a, eye2).reshape(g // 2, 2 * r, 2 * c)

    w1 = jnp.concatenate([pair_diag(a) for a in (m, ef_r, ef_i, eb_r, eb_i)], axis=2)
    w2 = jnp.concatenate([pair_diag(a) for a in (cf_r, cf_i, cb_r, cb_i)], axis=1)
    zeros = jnp.zeros((g // 2, 2 * SSM_STATE), F32)
    mult = jnp.stack([pw_r[t, 0].reshape(g // 2, -1), pw_i[t, 0].reshape(g // 2, -1),
                      pw_r[t, 1].reshape(g // 2, -1), pw_i[t, 1].reshape(g // 2, -1),
                      zeros, zeros, zeros, zeros], axis=1)
    return w1.astype(BF16), w2.astype(BF16), mult


def _s5_kernel(x_ref, w1_ref, w2_ref, a_ref, o_ref, r_ref, sp_ref, *, regions):
    ny = o_ref.shape[2]
    ns = 2 * SSM_STATE
    r_ref[...] = jnp.dot(x_ref[0], w1_ref[0], preferred_element_type=F32)
    a = a_ref[0]
    afr, afi, abr, abi = a[0:1], a[1:2], a[2:3], a[3:4]
    for base, nseq, nch in regions:
        fr = fi = br = bi = jnp.zeros((nseq, ns), F32)
        for c in range(nch):
            rf = slice(base + c * nseq, base + (c + 1) * nseq)
            cb = nch - 1 - c
            rb = slice(base + cb * nseq, base + (cb + 1) * nseq)
            sp_ref[rf, 0:ns] = fr
            sp_ref[rf, ns:2 * ns] = fi
            sp_ref[rb, 2 * ns:3 * ns] = br
            sp_ref[rb, 3 * ns:4 * ns] = bi
            er, ei = r_ref[rf, ny:ny + ns], r_ref[rf, ny + ns:ny + 2 * ns]
            gr, gi = r_ref[rb, ny + 2 * ns:ny + 3 * ns], r_ref[rb, ny + 3 * ns:ny + 4 * ns]
            fr, fi = afr * fr - afi * fi + er, afr * fi + afi * fr + ei
            br, bi = abr * br - abi * bi + gr, abr * bi + abi * br + gi
    o_ref[0] = r_ref[:, 0:ny] + jnp.dot(sp_ref[...].astype(BF16), w2_ref[0],
                                        preferred_element_type=F32)


def _s5_branch(u, seqs, params):
    n, width = u.shape
    g = width // SSM_GROUP
    t = S5_CHUNK
    w1, w2, mult = _s5_weights(*params)
    npair = g // 2
    parts, regions, row, tok = [], [], 0, 0
    for bsz, seqlen in seqs:
        nch = seqlen // t
        blk = u[tok:tok + bsz * seqlen].reshape(bsz, nch, t, npair, 2, SSM_GROUP)
        parts.append(jnp.transpose(blk, (3, 1, 0, 4, 2, 5)).reshape(npair, nch * bsz, 2 * t * SSM_GROUP))
        regions.append((row, bsz, nch))
        row += nch * bsz
        tok += bsz * seqlen
    xf = jnp.concatenate(parts, axis=1)
    nc = xf.shape[1]
    kin = 2 * t * SSM_GROUP
    ncol = w1.shape[2]
    y = pl.pallas_call(
        functools.partial(_s5_kernel, regions=tuple(regions)),
        grid=(npair,),
        in_specs=[
            pl.BlockSpec((1, nc, kin), lambda i: (i, 0, 0)),
            pl.BlockSpec((1, kin, ncol), lambda i: (i, 0, 0)),
            pl.BlockSpec((1, 4 * 2 * SSM_STATE, kin), lambda i: (i, 0, 0)),
            pl.BlockSpec((1, 8, 2 * SSM_STATE), lambda i: (i, 0, 0)),
        ],
        out_specs=pl.BlockSpec((1, nc, kin), lambda i: (i, 0, 0)),
        out_shape=jax.ShapeDtypeStruct((npair, nc, kin), F32),
        scratch_shapes=[pltpu.VMEM((nc, ncol), F32), pltpu.VMEM((nc, 4 * 2 * SSM_STATE), F32)],
        compiler_params=_cparams("parallel"),
        name="s5_chunks",
    )(xf, w1, w2, mult)
    outs = []
    for (row, bsz, nch), (_, seqlen) in zip(regions, seqs):
        blk = y[:, row:row + nch * bsz].reshape(npair, nch, bsz, 2, t, SSM_GROUP)
        outs.append(jnp.transpose(blk, (2, 1, 4, 0, 3, 5)).reshape(bsz * seqlen, width))
    return jnp.concatenate(outs, axis=0)


def _conv_kernel(vp_ref, gp_ref, vc_ref, gc_ref, vn_ref, gn_ref, w_ref, cb_ref, lg_ref, lb_ref,
                 o_ref, ext_ref, acc_ref, *, tl, bounds):
    r0 = pl.program_id(0) * tl
    pos, length = r0, bounds[0][1]
    for start, seqlen in bounds:
        inside = r0 >= start
        pos = jnp.where(inside, lax.rem(r0 - start, seqlen), pos)
        length = jnp.where(inside, seqlen, length)
    left_ok = (pos != 0).astype(F32)
    right_ok = (pos + tl != length).astype(F32)

    def glu(v_ref, g_ref):
        return v_ref[...].astype(F32) * _sigmoid(g_ref[...].astype(F32))

    ext_ref[0:HALO, :] = glu(vp_ref, gp_ref) * left_ok
    ext_ref[HALO:HALO + tl, :] = glu(vc_ref, gc_ref)
    ext_ref[HALO + tl:2 * HALO + tl, :] = glu(vn_ref, gn_ref) * right_ok

    cdim = o_ref.shape[1]
    rb = 64
    off = HALO - CONV_PAD

    def rows(i, carry):
        base = pl.multiple_of(i * rb, rb)
        for lb in range(cdim // LANES):
            lanes = slice(lb * LANES, (lb + 1) * LANES)
            win = ext_ref[pl.ds(base, rb + 2 * HALO), lanes]
            acc = jnp.zeros((rb, LANES), F32) + cb_ref[:, lanes]
            for k in range(CONV_WIDTH):
                acc = acc + w_ref[k:k + 1, lanes] * win[off + k:off + k + rb, :]
            acc_ref[pl.ds(base, rb), lanes] = acc
        return carry

    lax.fori_loop(0, tl // rb, rows, 0)
    hf = acc_ref[...]
    mu = jnp.mean(hf, axis=-1, keepdims=True)
    cen = hf - mu
    var = jnp.mean(cen * cen, axis=-1, keepdims=True)
    hn = cen * lax.rsqrt(var + EPS) * lg_ref[...] + lb_ref[...]
    o_ref[...] = (hn * _sigmoid(hn)).astype(o_ref.dtype)


def _conv_branch(z, col0, cdim, seqs, conv_w, conv_b, ln_g, ln_b):
    n = z.shape[0]
    tl = _tile(min(s for _, s in seqs), 256)
    hb = tl // HALO
    nhb = n // HALO
    bounds, tok = [], 0
    for bsz, seqlen in seqs:
        bounds.append((tok, seqlen))
        tok += bsz * seqlen
    cur = lambda c: pl.BlockSpec((tl, cdim), lambda i: (i, c))
    prev = lambda c: pl.BlockSpec((HALO, cdim), lambda i: (jnp.maximum(i * hb - 1, 0), c))
    nxt = lambda c: pl.BlockSpec((HALO, cdim), lambda i: (jnp.minimum((i + 1) * hb, nhb - 1), c))
    vec = pl.BlockSpec((1, cdim), lambda i: (0, 0))
    return pl.pallas_call(
        functools.partial(_conv_kernel, tl=tl, bounds=tuple(bounds)),
        grid=(n // tl,),
        in_specs=[prev(col0), prev(col0 + 1), cur(col0), cur(col0 + 1), nxt(col0), nxt(col0 + 1),
                  pl.BlockSpec((CONV_WIDTH, cdim), lambda i: (0, 0)), vec, vec, vec],
        out_specs=pl.BlockSpec((tl, cdim), lambda i: (i, 0)),
        out_shape=jax.ShapeDtypeStruct((n, cdim), BF16),
        scratch_shapes=[pltpu.VMEM((tl + 2 * HALO, cdim), F32), pltpu.VMEM((tl, cdim), F32)],
        compiler_params=_cparams("parallel"),
        name="conv_branch",
    )(z, z, z, z, z, z, conv_w, conv_b.reshape(1, cdim), ln_g.reshape(1, cdim), ln_b.reshape(1, cdim))


def _mixer_out_kernel(y_ref, cb_ref, ga_ref, gb_ref, x_ref, wglu_ref, bglu_ref, wa_ref, wb_ref,
                      wo_ref, o_ref):
    hg = _gelu(y_ref[...])
    gate = _sigmoid(jnp.dot(hg.astype(BF16), wglu_ref[...], preferred_element_type=F32)
                    + bglu_ref[...])
    a = jnp.dot((hg * gate).astype(BF16), wa_ref[...], preferred_element_type=F32)
    b = jnp.dot(cb_ref[...], wb_ref[...], preferred_element_type=F32)
    merged = ga_ref[...].astype(F32) * a + gb_ref[...].astype(F32) * b
    o_ref[...] = x_ref[...] + jnp.dot(merged.astype(BF16), wo_ref[...], preferred_element_type=F32)


def _mixer_out(y, cb, sg, x, w_glu, b_glu, w_a, w_b, w_o):
    n, d = x.shape
    sw = y.shape[1]
    cd = cb.shape[1]
    tm = _tile(n, 256)
    return pl.pallas_call(
        _mixer_out_kernel,
        grid=(n // tm,),
        in_specs=[
            pl.BlockSpec((tm, sw), lambda i: (i, 0)),
            pl.BlockSpec((tm, cd), lambda i: (i, 0)),
            pl.BlockSpec((tm, d), lambda i: (i, 0)),
            pl.BlockSpec((tm, d), lambda i: (i, 1)),
            pl.BlockSpec((tm, d), lambda i: (i, 0)),
            _resident(w_glu.shape), _resident((1, sw)), _resident(w_a.shape), _resident(w_b.shape),
            _resident(w_o.shape),
        ],
        out_specs=pl.BlockSpec((tm, d), lambda i: (i, 0)),
        out_shape=jax.ShapeDtypeStruct((n, d), F32),
        compiler_params=_cparams("parallel"),
        name="mixer_out",
    )(y, cb, sg, sg, x, w_glu, b_glu.reshape(1, sw), w_a, w_b, w_o)


def _top_rows(s, iota, count):
    nrows = s.shape[0]
    vals, idxs = [], []
    for _ in range(count):
        m = jnp.max(s, axis=0, keepdims=True)
        idx = jnp.min(jnp.where(s == m, iota, float(nrows)), axis=0, keepdims=True)
        vals.append(m)
        idxs.append(idx)
        s = jnp.where(iota == idx, -jnp.inf, s)
    return vals, idxs


def _peer_route_kernel(x_ref, g_ref, wq_ref, keys_ref, xn_ref, idx_ref, gate_ref, it_ref, gt_ref,
                       cs_ref, ce_ref):
    tm = x_ref.shape[0]
    xn = _rms(x_ref[...], g_ref[...])
    xn_ref[...] = xn
    q = jnp.dot(xn.astype(BF16), wq_ref[...], preferred_element_type=F32).astype(BF16)
    iota_k = lax.broadcasted_iota(I32, (N_KEYS, tm), 0).astype(F32)
    pairs = CAND_PAIRS
    ncand = N_CAND
    iota_c = lax.broadcasted_iota(I32, (ncand, tm), 0).astype(F32)
    for h in range(PEER_HEADS):
        tops = []
        for c in range(2):
            qc = q[:, (2 * h + c) * PEER_HALF:(2 * h + c + 1) * PEER_HALF]
            s = lax.dot_general(keys_ref[h, c], qc, (((1,), (1,)), ((), ())),
                                preferred_element_type=F32)
            tops.append(_top_rows(s, iota_k, PEER_TOPK))
        (s1, i1), (s2, i2) = tops
        cs_ref[...] = jnp.full((ncand, tm), -jnp.inf, F32)
        ce_ref[...] = jnp.zeros((ncand, tm), F32)
        for r, (i, j) in enumerate(pairs):
            cs_ref[r:r + 1, :] = s1[i] + s2[j]
            ce_ref[r:r + 1, :] = i1[i] * float(N_KEYS) + i2[j]
        cs = cs_ref[...]
        ce = ce_ref[...]
        best = []
        for k in range(PEER_TOPK):
            m = jnp.max(cs, axis=0, keepdims=True)
            pos = jnp.min(jnp.where(cs == m, iota_c, float(ncand)), axis=0, keepdims=True)
            hit = iota_c == pos
            it_ref[h * PEER_TOPK + k:h * PEER_TOPK + k + 1, :] = jnp.max(
                jnp.where(hit, ce, -1.0), axis=0, keepdims=True)
            best.append(m)
            cs = jnp.where(hit, -jnp.inf, cs)
        ex = [jnp.exp(b - best[0]) for b in best]
        inv = 1.0 / functools.reduce(lambda p, r: p + r, ex)
        for k in range(PEER_TOPK):
            gt_ref[h * PEER_TOPK + k:h * PEER_TOPK + k + 1, :] = ex[k] * inv
    idx_ref[...] = it_ref[...].T.astype(I32)
    gate_ref[...] = gt_ref[...].T


def _peer_route(x, g, w_q, keys):
    n, d = x.shape
    tm = _tile(n, 256)
    qcols = w_q.shape[1]
    return pl.pallas_call(
        _peer_route_kernel,
        grid=(n // tm,),
        in_specs=[
            pl.BlockSpec((tm, d), lambda i: (i, 0)),
            _resident((1, d)), _resident((d, qcols)), _resident(keys.shape),
        ],
        out_specs=[
            pl.BlockSpec((tm, d), lambda i: (i, 0)),
            pl.BlockSpec((tm, HK), lambda i: (i, 0)),
            pl.BlockSpec((tm, HK), lambda i: (i, 0)),
        ],
        out_shape=[
            jax.ShapeDtypeStruct((n, d), F32),
            jax.ShapeDtypeStruct((n, HK), I32),
            jax.ShapeDtypeStruct((n, HK), F32),
        ],
        scratch_shapes=[pltpu.VMEM((HK, tm), F32), pltpu.VMEM((HK, tm), F32),
                        pltpu.VMEM((N_CAND, tm), F32), pltpu.VMEM((N_CAND, tm), F32)],
        compiler_params=_cparams("parallel"),
        name="peer_route",
    )(x, g.reshape(1, d), w_q, keys)


EXPERT_SLOTS = 4
TOKEN_GROUP = 8


def _pack_expert_tables(u_tab, v_tab):
    hi = lax.bitcast_convert_type(u_tab.astype(BF16), jnp.uint16).astype(jnp.uint32) << 16
    lo = lax.bitcast_convert_type(v_tab.astype(BF16), jnp.uint16).astype(jnp.uint32)
    return lax.bitcast_convert_type(hi | lo, I32)


def _peer_experts_kernel(idx_ref, gate_ref, xn_ref, x_ref, tab_ref, o_ref, buf_ref, sem_ref):
    tb, d = xn_ref.shape
    nlb = d // LANES

    def row_copy(t, k, slot):
        e = idx_ref[t, k]
        return pltpu.make_async_copy(tab_ref.at[pl.ds(e, 1), :],
                                     buf_ref.at[slot, pl.ds(k, 1), :], sem_ref.at[slot])

    def issue(t, slot):
        for k in range(HK):
            row_copy(t, k, slot).start()

    def wait(slot):
        pltpu.make_async_copy(tab_ref.at[pl.ds(0, HK), :], buf_ref.at[slot], sem_ref.at[slot]).wait()

    lookahead = EXPERT_SLOTS - 1
    for t in range(lookahead):
        issue(t, t)

    row_iota = lax.broadcasted_iota(I32, (TOKEN_GROUP, LANES), 0)

    def group(g, carry):
        t0 = pl.multiple_of(g * TOKEN_GROUP, TOKEN_GROUP)
        rows = pl.ds(t0, TOKEN_GROUP)
        xn8 = xn_ref[rows, :]
        gate8 = gate_ref[rows, :]
        outs = [x_ref[rows, j * LANES:(j + 1) * LANES] for j in range(nlb)]
        for i in range(TOKEN_GROUP):
            slot = i % EXPERT_SLOTS
            ahead = t0 + i + lookahead

            @pl.when(ahead < tb)
            def _():
                issue(ahead, (i + lookahead) % EXPERT_SLOTS)

            wait(slot)
            acc = jnp.zeros((HK, LANES), F32)
            for j in range(nlb):
                lanes = slice(j * LANES, (j + 1) * LANES)
                w = buf_ref[slot, :, lanes]
                u = lax.bitcast_convert_type(w & jnp.int32(-65536), F32)
                acc = acc + u * xn8[i:i + 1, lanes]
            s = jnp.sum(acc.T, axis=0, keepdims=True)
            act = _gelu(s) * gate8[i:i + 1, :]
            act_col = jnp.broadcast_to(act, (LANES, HK)).T
            for j in range(nlb):
                lanes = slice(j * LANES, (j + 1) * LANES)
                w = buf_ref[slot, :, lanes]
                v = lax.bitcast_convert_type(w << 16, F32)
                y = jnp.sum(v * act_col, axis=0, keepdims=True)
                outs[j] = jnp.where(row_iota == i, outs[j] + y, outs[j])
        for j in range(nlb):
            o_ref[rows, j * LANES:(j + 1) * LANES] = outs[j]
        return carry

    lax.fori_loop(0, tb // TOKEN_GROUP, group, 0)


def _peer_experts(idx, gates, xn, x, table):
    n, d = x.shape
    tb = _tile(n, 128)
    return pl.pallas_call(
        _peer_experts_kernel,
        grid=(n // tb,),
        in_specs=[
            pl.BlockSpec((tb, HK), lambda i: (i, 0), memory_space=pltpu.SMEM),
            pl.BlockSpec((tb, HK), lambda i: (i, 0)),
            pl.BlockSpec((tb, d), lambda i: (i, 0)),
            pl.BlockSpec((tb, d), lambda i: (i, 0)),
            pl.BlockSpec(memory_space=pl.ANY),
        ],
        out_specs=pl.BlockSpec((tb, d), lambda i: (i, 0)),
        out_shape=jax.ShapeDtypeStruct((n, d), F32),
        scratch_shapes=[pltpu.VMEM((EXPERT_SLOTS, HK, d), I32),
                        pltpu.SemaphoreType.DMA((EXPERT_SLOTS,))],
        compiler_params=_cparams("arbitrary"),
        name="peer_experts",
    )(idx, gates, xn, x, table)


def _ple_final_kernel(x_ref, p_ref, gp_ref, wg_ref, bg_ref, wp_ref, gf_ref, o_ref, *, final):
    x = x_ref[...]
    pg = _sigmoid(jnp.dot(_rms(x, gp_ref[...]).astype(BF16), wg_ref[...],
                          preferred_element_type=F32) + bg_ref[...])
    pe = jnp.dot(p_ref[...].astype(BF16), wp_ref[...], preferred_element_type=F32)
    y = x + pe * pg
    o_ref[...] = _rms(y, gf_ref[...]) if final else y


def _ple_final(x, p, g_ple, w_gate, b_gate, w_ple, g_final, final):
    n, d = x.shape
    pd = p.shape[1]
    tm = _tile(n, 256)
    return pl.pallas_call(
        functools.partial(_ple_final_kernel, final=final),
        grid=(n // tm,),
        in_specs=[
            pl.BlockSpec((tm, d), lambda i: (i, 0)),
            pl.BlockSpec((tm, pd), lambda i: (i, 0)),
            _resident((1, d)), _resident(w_gate.shape), _resident((1, d)), _resident(w_ple.shape),
            _resident((1, d)),
        ],
        out_specs=pl.BlockSpec((tm, d), lambda i: (i, 0)),
        out_shape=jax.ShapeDtypeStruct((n, d), F32),
        compiler_params=_cparams("parallel"),
        name="ple_final",
    )(x, p, g_ple.reshape(1, d), w_gate, b_gate.reshape(1, d), w_ple, g_final.reshape(1, d))


def kernel(x_prompt, x_sample, p_prompt, p_sample, norm_mix_g, w_in, w_gate, b_gate, ssm_lam_re, ssm_lam_im, ssm_b_re, ssm_b_im, ssm_c_re, ssm_c_im, ssm_log_step, ssm_d, ssm_w_glu, ssm_b_glu, ssm_w_out, conv_w, conv_b, conv_ln_g, conv_ln_b, conv_w_out, w_o, norm_ffn_g, peer_w_q, peer_keys, peer_u, peer_v, norm_ple_g, ple_w, ple_w_gate, ple_b_gate, norm_final_g):
    depth = w_in.shape[0]
    d = x_prompt.shape[-1]
    seqs = (x_prompt.shape[:2], x_sample.shape[:2])
    n_p = seqs[0][0] * seqs[0][1]
    x = jnp.concatenate([x_prompt.reshape(-1, d), x_sample.reshape(-1, d)], axis=0)
    sw = ssm_w_glu.shape[1]
    cdim = conv_w.shape[2]
    assert sw == cdim and w_in.shape[2] == sw + 2 * cdim and sw % (2 * SSM_GROUP) == 0
    for i in range(depth):
        p = jnp.concatenate([p_prompt[i].reshape(n_p, -1), p_sample[i].reshape(x.shape[0] - n_p, -1)], axis=0)
        z = _norm_matmul(x, norm_mix_g[i], w_in[i].astype(BF16), jnp.zeros((w_in.shape[2],), F32),
                         "none", BF16)
        sg = _norm_matmul(x, norm_mix_g[i], w_gate[i].astype(BF16), b_gate[i], "sigmoid", BF16)
        y = _s5_branch(z[:, :sw], seqs,
                       (ssm_lam_re[i], ssm_lam_im[i], ssm_b_re[i], ssm_b_im[i], ssm_c_re[i],
                        ssm_c_im[i], ssm_log_step[i], ssm_d[i]))
        cb = _conv_branch(z, sw // cdim, cdim, seqs, conv_w[i], conv_b[i], conv_ln_g[i], conv_ln_b[i])
        x = _mixer_out(y, cb, sg, x, ssm_w_glu[i].astype(BF16), ssm_b_glu[i], ssm_w_out[i].astype(BF16),
                       conv_w_out[i].astype(BF16), w_o[i].astype(BF16))
        xn, idx, gates = _peer_route(x, norm_ffn_g[i], peer_w_q[i].astype(BF16), peer_keys[i].astype(BF16))
        x = _peer_experts(idx, gates, xn, x, _pack_expert_tables(peer_u[i], peer_v[i]))
        x = _ple_final(x, p, norm_ple_g[i], ple_w_gate[i].astype(BF16), ple_b_gate[i],
                       ple_w[i].astype(BF16), norm_final_g, i == depth - 1)
    y_prompt = x[:n_p].reshape(x_prompt.shape)
    y_sample = x[n_p:].reshape(x_sample.shape)
    return (y_prompt, y_sample)
```

```python
import functools
import math

import jax
import jax.numpy as jnp
from jax import lax
from jax.experimental import pallas as pl
from jax.experimental.pallas import tpu as pltpu

F32 = jnp.float32
BF16 = jnp.bfloat16
I32 = jnp.int32

EPS = 1e-6
LANES = 128
SSM_GROUP = 16
SSM_STATE = 64
S5_CHUNK = 16
CONV_WIDTH = 31
CONV_PAD = CONV_WIDTH // 2
HALO = 16
PEER_HEADS = 8
PEER_HALF = 128
N_KEYS = 128
PEER_TOPK = 16
HK = PEER_HEADS * PEER_TOPK
CAND_PAIRS = tuple((i, j) for i in range(PEER_TOPK) for j in range(PEER_TOPK // (i + 1)))
N_CAND = -(-len(CAND_PAIRS) // 8) * 8
VMEM_LIMIT = 56 * 1024 * 1024


def _cparams(*sem):
    return pltpu.CompilerParams(dimension_semantics=sem, vmem_limit_bytes=VMEM_LIMIT)


def _tile(n, pref):
    t = min(n, pref)
    while n % t:
        t //= 2
    return t


def _sigmoid(x):
    return 1.0 / (1.0 + jnp.exp(-x))


def _gelu(x):
    return 0.5 * x * (1.0 + lax.erf(x * (1.0 / math.sqrt(2.0))))


def _rms(x, g):
    return x * lax.rsqrt(jnp.mean(x * x, axis=-1, keepdims=True) + EPS) * g


def _resident(shape):
    nd = len(shape)
    return pl.BlockSpec(shape, lambda *_: (0,) * nd, pipeline_mode=pl.Buffered(1))


def _norm_matmul_kernel(x_ref, g_ref, w_ref, b_ref, o_ref, h_ref, *, act):
    @pl.when(pl.program_id(1) == 0)
    def _():
        h_ref[...] = _rms(x_ref[...], g_ref[...]).astype(BF16)

    y = jnp.dot(h_ref[...], w_ref[...], preferred_element_type=F32) + b_ref[...]
    if act == "sigmoid":
        y = _sigmoid(y)
    o_ref[...] = y.astype(o_ref.dtype)


def _norm_matmul(x, g, w, b, act, out_dtype):
    n, d = x.shape
    cols = w.shape[1]
    tm = _tile(n, 512)
    tn = _tile(cols, 1024)
    return pl.pallas_call(
        functools.partial(_norm_matmul_kernel, act=act),
        grid=(n // tm, cols // tn),
        in_specs=[
            pl.BlockSpec((tm, d), lambda i, j: (i, 0)),
            pl.BlockSpec((1, d), lambda i, j: (0, 0)),
            pl.BlockSpec((d, tn), lambda i, j: (0, j)),
            pl.BlockSpec((1, tn), lambda i, j: (0, j)),
        ],
        out_specs=pl.BlockSpec((tm, tn), lambda i, j: (i, j)),
        out_shape=jax.ShapeDtypeStruct((n, cols), out_dtype),
        scratch_shapes=[pltpu.VMEM((tm, d), BF16)],
        compiler_params=_cparams("parallel", "arbitrary"),
        name="norm_matmul",
    )(x, g.reshape(1, d), w, b.reshape(1, cols))


def _s5_weights(lam_re, lam_im, b_re, b_im, c_re, c_im, log_step, d_skip):
    t = S5_CHUNK
    g = lam_re.shape[1]
    step = jnp.exp(log_step)[..., None]
    zr, zi = lam_re * step, lam_im * step
    dd = jnp.arange(t + 1, dtype=F32).reshape(t + 1, 1, 1, 1)
    mag = jnp.exp(dd * zr)
    pw_r, pw_i = mag * jnp.cos(dd * zi), mag * jnp.sin(dd * zi)
    nr, ni = pw_r[1] - 1.0, pw_i[1]
    den = lam_re * lam_re + lam_im * lam_im
    qr = (nr * lam_re + ni * lam_im) / den
    qi = (ni * lam_re - nr * lam_im) / den
    bb_r = qr[..., None] * b_re - qi[..., None] * b_im
    bb_i = qr[..., None] * b_im + qi[..., None] * b_re
    lb_r = pw_r[:t, ..., None] * bb_r - pw_i[:t, ..., None] * bb_i
    lb_i = pw_r[:t, ..., None] * bb_i + pw_i[:t, ..., None] * bb_r
    hi = lax.Precision.HIGHEST
    kern = (jnp.einsum("xgkp,dxgph->xgdkh", c_re, lb_r, precision=hi)
            - jnp.einsum("xgkp,dxgph->xgdkh", c_im, lb_i, precision=hi))
    tau = jnp.arange(t)[:, None]
    tt = jnp.arange(t)[None, :]
    mf = kern[0][:, jnp.clip(tt - tau, 0, t - 1)] * (tt >= tau)[None, :, :, None, None]
    mb = kern[1][:, jnp.clip(tau - tt, 0, t - 1)] * (tau >= tt)[None, :, :, None, None]
    m = jnp.transpose(mf + mb, (0, 1, 4, 2, 3))
    skip = (d_skip.reshape(g, 1, SSM_GROUP, 1, 1)
            * jnp.eye(t, dtype=F32).reshape(1, t, 1, t, 1)
            * jnp.eye(SSM_GROUP, dtype=F32).reshape(1, 1, SSM_GROUP, 1, SSM_GROUP))
    m = m + skip
    rev = (t - 1) - jnp.arange(t)
    ef_r = jnp.transpose(lb_r[rev, 0], (1, 0, 3, 2))
    ef_i = jnp.transpose(lb_i[rev, 0], (1, 0, 3, 2))
    eb_r = jnp.transpose(lb_r[:, 1], (1, 0, 3, 2))
    eb_i = jnp.transpose(lb_i[:, 1], (1, 0, 3, 2))
    pf_r, pf_i = pw_r[1:, 0], pw_i[1:, 0]
    pb_r, pb_i = pw_r[rev + 1, 1], pw_i[rev + 1, 1]

    def carry_rows(cr, ci, pr, pi):
        re = cr[None] * pr[:, :, None, :] - ci[None] * pi[:, :, None, :]
        im = cr[None] * pi[:, :, None, :] + ci[None] * pr[:, :, None, :]
        return jnp.transpose(re, (1, 3, 0, 2)), -jnp.transpose(im, (1, 3, 0, 2))

    cf_r, cf_i = carry_rows(c_re[0], c_im[0], pf_r, pf_i)
    cb_r, cb_i = carry_rows(c_re[1], c_im[1], pb_r, pb_i)

    gb = LANES // SSM_GROUP
    nb = g // gb
    eye = jnp.eye(gb, dtype=F32)
    hh, pp = SSM_GROUP, SSM_STATE

    def state_cols(e):
        e = jnp.einsum("bnxhp,nm->bxnhmp", e.reshape(nb, gb, t, hh, pp), eye)
        return e.reshape(nb, t * LANES, gb * pp)

    def carry_block(c):
        c = jnp.einsum("bnpty,nm->bnptmy", c.reshape(nb, gb, pp, t, hh), eye)
        return c.reshape(nb, gb * pp, t * LANES)

    w1y = jnp.einsum("bnxhty,nm->bxnhtmy", m.reshape(nb, gb, t, hh, t, hh), eye)
    w1 = jnp.concatenate([w1y.reshape(nb, t * LANES, t * LANES)]
                         + [state_cols(e.reshape(g, t, hh, pp)) for e in (ef_r, ef_i, eb_r, eb_i)], axis=2)
    w2 = jnp.concatenate([carry_block(c.reshape(g, pp, t, hh)) for c in (cf_r, cf_i, cb_r, cb_i)], axis=1)
    zeros = jnp.zeros((nb, gb * pp), F32)
    mult = jnp.stack([pw_r[t, 0].reshape(nb, -1), pw_i[t, 0].reshape(nb, -1),
                      pw_r[t, 1].reshape(nb, -1), pw_i[t, 1].reshape(nb, -1),
                      zeros, zeros, zeros, zeros], axis=1)
    return w1.astype(BF16), w2.astype(BF16), mult


def _s5_kernel(u_ref, w1_ref, w2_ref, a_ref, o_ref, r_ref, sp_ref, *, variants):
    t = S5_CHUNK
    nc = u_ref.shape[0] // t
    ny = t * LANES
    ns = sp_ref.shape[1] // 4
    x = jnp.concatenate([u_ref[pl.ds(tau, nc, stride=t), :].astype(BF16) for tau in range(t)], axis=1)
    r_ref[...] = jnp.dot(x, w1_ref[0], preferred_element_type=F32)
    a = a_ref[0]
    afr, afi, abr, abi = a[0:1], a[1:2], a[2:3], a[3:4]

    def scan(nseq, nch):
        zero = jnp.zeros((1, ns), F32)
        state = [(zero, zero, zero, zero)] * nseq
        for c in range(nch):
            for s in range(nseq):
                fr, fi, br, bi = state[s]
                rf = slice(s * nch + c, s * nch + c + 1)
                rb = slice(s * nch + nch - 1 - c, s * nch + nch - c)
                sp_ref[rf, 0:ns] = fr
                sp_ref[rf, ns:2 * ns] = fi
                sp_ref[rb, 2 * ns:3 * ns] = br
                sp_ref[rb, 3 * ns:4 * ns] = bi
                er, ei = r_ref[rf, ny:ny + ns], r_ref[rf, ny + ns:ny + 2 * ns]
                gr, gi = r_ref[rb, ny + 2 * ns:ny + 3 * ns], r_ref[rb, ny + 3 * ns:ny + 4 * ns]
                state[s] = (afr * fr - afi * fi + er, afr * fi + afi * fr + ei,
                            abr * br - abi * bi + gr, abr * bi + abi * br + gi)

    if len(variants) == 1:
        scan(*variants[0][2:])
    else:
        blk = pl.program_id(1)
        for lo, hi, nseq, nch in variants:
            pl.when((blk >= lo) & (blk < hi))(functools.partial(scan, nseq, nch))
    ytot = r_ref[:, 0:ny] + jnp.dot(sp_ref[...].astype(BF16), w2_ref[0], preferred_element_type=F32)
    for tt in range(t):
        o_ref[pl.ds(tt, nc, stride=t), :] = ytot[:, tt * LANES:(tt + 1) * LANES]


def _s5_branch(u, seqs, params):
    n, width = u.shape
    t = S5_CHUNK
    w1, w2, mult = _s5_weights(*params)
    nb = width // LANES
    rblk = max(s for _, s in seqs)
    variants, blk = [], 0
    for bsz, seqlen in seqs:
        assert rblk % seqlen == 0 and (bsz * seqlen) % rblk == 0 and seqlen % t == 0
        nblk = bsz * seqlen // rblk
        variants.append((blk, blk + nblk, rblk // seqlen, seqlen // t))
        blk += nblk
    if all(v[2:] == variants[0][2:] for v in variants):
        variants = [(0, blk) + variants[0][2:]]
    nc = rblk // t
    nst = w1.shape[2] - t * LANES
    return pl.pallas_call(
        functools.partial(_s5_kernel, variants=tuple(variants)),
        grid=(nb, n // rblk),
        in_specs=[
            pl.BlockSpec((rblk, LANES), lambda b, r: (r, b)),
            pl.BlockSpec((1,) + w1.shape[1:], lambda b, r: (b, 0, 0), pipeline_mode=pl.Buffered(1)),
            pl.BlockSpec((1,) + w2.shape[1:], lambda b, r: (b, 0, 0), pipeline_mode=pl.Buffered(1)),
            pl.BlockSpec((1,) + mult.shape[1:], lambda b, r: (b, 0, 0)),
        ],
        out_specs=pl.BlockSpec((rblk, LANES), lambda b, r: (r, b)),
        out_shape=jax.ShapeDtypeStruct((n, width), F32),
        scratch_shapes=[pltpu.VMEM((nc, w1.shape[2]), F32), pltpu.VMEM((nc, nst), F32)],
        compiler_params=_cparams("parallel", "parallel"),
        name="s5_chunks",
    )(u, w1, w2, mult)


def _conv_kernel(vp_ref, gp_ref, vc_ref, gc_ref, vn_ref, gn_ref, w_ref, cb_ref, lg_ref, lb_ref,
                 o_ref, ext_ref, acc_ref, *, tl, bounds):
    r0 = pl.program_id(0) * tl
    pos, length = r0, bounds[0][1]
    for start, seqlen in bounds:
        inside = r0 >= start
        pos = jnp.where(inside, lax.rem(r0 - start, seqlen), pos)
        length = jnp.where(inside, seqlen, length)
    left_ok = (pos != 0).astype(F32)
    right_ok = (pos + tl != length).astype(F32)

    def glu(v_ref, g_ref):
        return v_ref[...].astype(F32) * _sigmoid(g_ref[...].astype(F32))

    ext_ref[0:HALO, :] = glu(vp_ref, gp_ref) * left_ok
    ext_ref[HALO:HALO + tl, :] = glu(vc_ref, gc_ref)
    ext_ref[HALO + tl:2 * HALO + tl, :] = glu(vn_ref, gn_ref) * right_ok

    cdim = o_ref.shape[1]
    rb = 64
    off = HALO - CONV_PAD

    def rows(i, carry):
        base = pl.multiple_of(i * rb, rb)
        for lb in range(cdim // LANES):
            lanes = slice(lb * LANES, (lb + 1) * LANES)
            win = ext_ref[pl.ds(base, rb + 2 * HALO), lanes]
            acc = jnp.zeros((rb, LANES), F32) + cb_ref[:, lanes]
            for k in range(CONV_WIDTH):
                acc = acc + w_ref[k:k + 1, lanes] * win[off + k:off + k + rb, :]
            acc_ref[pl.ds(base, rb), lanes] = acc
        return carry

    lax.fori_loop(0, tl // rb, rows, 0)
    hf = acc_ref[...]
    mu = jnp.mean(hf, axis=-1, keepdims=True)
    cen = hf - mu
    var = jnp.mean(cen * cen, axis=-1, keepdims=True)
    hn = cen * lax.rsqrt(var + EPS) * lg_ref[...] + lb_ref[...]
    o_ref[...] = (hn * _sigmoid(hn)).astype(o_ref.dtype)


def _conv_branch(z, col0, cdim, seqs, conv_w, conv_b, ln_g, ln_b):
    n = z.shape[0]
    tl = _tile(min(s for _, s in seqs), 256)
    hb = tl // HALO
    nhb = n // HALO
    bounds, tok = [], 0
    for bsz, seqlen in seqs:
        bounds.append((tok, seqlen))
        tok += bsz * seqlen
    cur = lambda c: pl.BlockSpec((tl, cdim), lambda i: (i, c))
    prev = lambda c: pl.BlockSpec((HALO, cdim), lambda i: (jnp.maximum(i * hb - 1, 0), c))
    nxt = lambda c: pl.BlockSpec((HALO, cdim), lambda i: (jnp.minimum((i + 1) * hb, nhb - 1), c))
    vec = pl.BlockSpec((1, cdim), lambda i: (0, 0))
    return pl.pallas_call(
        functools.partial(_conv_kernel, tl=tl, bounds=tuple(bounds)),
        grid=(n // tl,),
        in_specs=[prev(col0), prev(col0 + 1), cur(col0), cur(col0 + 1), nxt(col0), nxt(col0 + 1),
                  pl.BlockSpec((CONV_WIDTH, cdim), lambda i: (0, 0)), vec, vec, vec],
        out_specs=pl.BlockSpec((tl, cdim), lambda i: (i, 0)),
        out_shape=jax.ShapeDtypeStruct((n, cdim), BF16),
        scratch_shapes=[pltpu.VMEM((tl + 2 * HALO, cdim), F32), pltpu.VMEM((tl, cdim), F32)],
        compiler_params=_cparams("parallel"),
        name="conv_branch",
    )(z, z, z, z, z, z, conv_w, conv_b.reshape(1, cdim), ln_g.reshape(1, cdim), ln_b.reshape(1, cdim))


def _mixer_out_kernel(y_ref, cb_ref, ga_ref, gb_ref, x_ref, wglu_ref, bglu_ref, wa_ref, wb_ref,
                      wo_ref, o_ref):
    hg = _gelu(y_ref[...])
    gate = _sigmoid(jnp.dot(hg.astype(BF16), wglu_ref[...], preferred_element_type=F32)
                    + bglu_ref[...])
    a = jnp.dot((hg * gate).astype(BF16), wa_ref[...], preferred_element_type=F32)
    b = jnp.dot(cb_ref[...], wb_ref[...], preferred_element_type=F32)
    merged = ga_ref[...].astype(F32) * a + gb_ref[...].astype(F32) * b
    o_ref[...] = x_ref[...] + jnp.dot(merged.astype(BF16), wo_ref[...], preferred_element_type=F32)


def _mixer_out(y, cb, sg, x, w_glu, b_glu, w_a, w_b, w_o):
    n, d = x.shape
    sw = y.shape[1]
    cd = cb.shape[1]
    tm = _tile(n, 256)
    return pl.pallas_call(
        _mixer_out_kernel,
        grid=(n // tm,),
        in_specs=[
            pl.BlockSpec((tm, sw), lambda i: (i, 0)),
            pl.BlockSpec((tm, cd), lambda i: (i, 0)),
            pl.BlockSpec((tm, d), lambda i: (i, 0)),
            pl.BlockSpec((tm, d), lambda i: (i, 1)),
            pl.BlockSpec((tm, d), lambda i: (i, 0)),
            _resident(w_glu.shape), _resident((1, sw)), _resident(w_a.shape), _resident(w_b.shape),
            _resident(w_o.shape),
        ],
        out_specs=pl.BlockSpec((tm, d), lambda i: (i, 0)),
        out_shape=jax.ShapeDtypeStruct((n, d), F32),
        compiler_params=_cparams("parallel"),
        name="mixer_out",
    )(y, cb, sg, sg, x, w_glu, b_glu.reshape(1, sw), w_a, w_b, w_o)


def _top_rows(s, iota, count):
    nrows = s.shape[0]
    vals, idxs = [], []
    for _ in range(count):
        m = jnp.max(s, axis=0, keepdims=True)
        idx = jnp.min(jnp.where(s == m, iota, float(nrows)), axis=0, keepdims=True)
        vals.append(m)
        idxs.append(idx)
        s = jnp.where(iota == idx, -jnp.inf, s)
    return vals, idxs


def _peer_route_kernel(x_ref, g_ref, wq_ref, keys_ref, xn_ref, idx_ref, gate_ref, it_ref, gt_ref,
                       cs_ref, ce_ref):
    tm = x_ref.shape[0]
    xn = _rms(x_ref[...], g_ref[...])
    xn_ref[...] = xn
    q = jnp.dot(xn.astype(BF16), wq_ref[...], preferred_element_type=F32).astype(BF16)
    iota_k = lax.broadcasted_iota(I32, (N_KEYS, tm), 0).astype(F32)
    pairs = CAND_PAIRS
    ncand = N_CAND
    iota_c = lax.broadcasted_iota(I32, (ncand, tm), 0).astype(F32)
    for h in range(PEER_HEADS):
        tops = []
        for c in range(2):
            qc = q[:, (2 * h + c) * PEER_HALF:(2 * h + c + 1) * PEER_HALF]
            s = lax.dot_general(keys_ref[h, c], qc, (((1,), (1,)), ((), ())),
                                preferred_element_type=F32)
            tops.append(_top_rows(s, iota_k, PEER_TOPK))
        (s1, i1), (s2, i2) = tops
        cs_ref[...] = jnp.full((ncand, tm), -jnp.inf, F32)
        ce_ref[...] = jnp.zeros((ncand, tm), F32)
        for r, (i, j) in enumerate(pairs):
            cs_ref[r:r + 1, :] = s1[i] + s2[j]
            ce_ref[r:r + 1, :] = i1[i] * float(N_KEYS) + i2[j]
        cs = cs_ref[...]
        ce = ce_ref[...]
        best = []
        for k in range(PEER_TOPK):
            m = jnp.max(cs, axis=0, keepdims=True)
            pos = jnp.min(jnp.where(cs == m, iota_c, float(ncand)), axis=0, keepdims=True)
            hit = iota_c == pos
            it_ref[h * PEER_TOPK + k:h * PEER_TOPK + k + 1, :] = jnp.max(
                jnp.where(hit, ce, -1.0), axis=0, keepdims=True)
            best.append(m)
            cs = jnp.where(hit, -jnp.inf, cs)
        ex = [jnp.exp(b - best[0]) for b in best]
        inv = 1.0 / functools.reduce(lambda p, r: p + r, ex)
        for k in range(PEER_TOPK):
            gt_ref[h * PEER_TOPK + k:h * PEER_TOPK + k + 1, :] = ex[k] * inv
    idx_ref[...] = it_ref[...].T.astype(I32)
    gate_ref[...] = gt_ref[...].T


def _peer_route(x, g, w_q, keys):
    n, d = x.shape
    tm = _tile(n, 256)
    qcols = w_q.shape[1]
    return pl.pallas_call(
        _peer_route_kernel,
        grid=(n // tm,),
        in_specs=[
            pl.BlockSpec((tm, d), lambda i: (i, 0)),
            _resident((1, d)), _resident((d, qcols)), _resident(keys.shape),
        ],
        out_specs=[
            pl.BlockSpec((tm, d), lambda i: (i, 0)),
            pl.BlockSpec((tm, HK), lambda i: (i, 0)),
            pl.BlockSpec((tm, HK), lambda i: (i, 0)),
        ],
        out_shape=[
            jax.ShapeDtypeStruct((n, d), F32),
            jax.ShapeDtypeStruct((n, HK), I32),
            jax.ShapeDtypeStruct((n, HK), F32),
        ],
        scratch_shapes=[pltpu.VMEM((HK, tm), F32), pltpu.VMEM((HK, tm), F32),
                        pltpu.VMEM((N_CAND, tm), F32), pltpu.VMEM((N_CAND, tm), F32)],
        compiler_params=_cparams("parallel"),
        name="peer_route",
    )(x, g.reshape(1, d), w_q, keys)


EXPERT_SLOTS = 8
LOOKAHEAD = EXPERT_SLOTS - 2
TOKEN_GROUP = 16


def _pack_tables_kernel(u_ref, v_ref, o_ref):
    packed = pltpu.pack_elementwise([v_ref[...], u_ref[...]], packed_dtype=BF16)
    for r in range(packed.shape[0]):
        o_ref[r] = packed[r:r + 1, :]


def _pack_expert_tables(u_tab, v_tab):
    e, d = u_tab.shape
    te = _tile(e, 64)
    return pl.pallas_call(
        _pack_tables_kernel,
        grid=(e // te,),
        in_specs=[pl.BlockSpec((te, d), lambda i: (i, 0)), pl.BlockSpec((te, d), lambda i: (i, 0))],
        out_specs=pl.BlockSpec((te, 1, d), lambda i: (i, 0, 0)),
        out_shape=jax.ShapeDtypeStruct((e, 1, d), jnp.uint32),
        compiler_params=_cparams("parallel"),
        name="pack_tables",
    )(u_tab, v_tab)


def _peer_experts_kernel(idx_ref, idx_next_ref, gate_ref, xn_ref, x_ref, tab_ref, o_ref, *scratch):
    bufs = scratch[:EXPERT_SLOTS]
    sem_ref, xs_ref, os_ref = scratch[EXPERT_SLOTS:]
    tb, d = xn_ref.shape
    nlb = d // LANES
    ngroups = tb // TOKEN_GROUP
    step = pl.program_id(0)

    def issuer(ref, row, slot):
        ks = iter(range(HK))

        def emit(n):
            for k in [k for _, k in zip(range(n), ks)]:
                pltpu.make_async_copy(tab_ref.at[ref[row, k]],
                                      bufs[slot].at[pl.ds(k, 1), :], sem_ref.at[slot]
                                      ).start(priority=k % 2)
        return emit

    def issue(ref, row, slot):
        issuer(ref, row, slot)(HK)

    def wait(slot):
        pltpu.make_async_copy(bufs[slot], bufs[slot], sem_ref.at[slot]).wait()

    @pl.when(step == 0)
    def _():
        for t in range(LOOKAHEAD):
            issue(idx_ref, t, t)

    def first_half(i, slot, gate, emit, per):
        acc = jnp.zeros((HK, LANES), F32)
        for j in range(nlb):
            lanes = slice(j * LANES, (j + 1) * LANES)
            u = pltpu.unpack_elementwise(bufs[slot][:, lanes], index=1, packed_dtype=BF16,
                                         unpacked_dtype=F32)
            acc = acc + u * xs_ref[i:i + 1, lanes]
            emit(per)
        s = jnp.sum(acc.T, axis=0, keepdims=True)
        act = _gelu(s) * gate
        return jnp.broadcast_to(act, (LANES, HK)).T

    def second_half(i, slot, act_col, emit, per):
        for j in range(nlb):
            lanes = slice(j * LANES, (j + 1) * LANES)
            v = pltpu.unpack_elementwise(bufs[slot][:, lanes], index=0, packed_dtype=BF16,
                                         unpacked_dtype=F32)
            os_ref[i:i + 1, lanes] += jnp.sum(v * act_col, axis=0, keepdims=True)
            emit(per)

    def group(t0, last):
        rows = pl.ds(t0, TOKEN_GROUP)
        xs_ref[...] = xn_ref[rows, :]
        os_ref[...] = x_ref[rows, :]
        gates = gate_ref[rows, :]
        act_col = None
        for i in range(TOKEN_GROUP):
            slot = i % EXPERT_SLOTS
            wait(slot)
            ahead = i + LOOKAHEAD
            if last and ahead >= TOKEN_GROUP:
                emit = issuer(idx_next_ref, ahead - TOKEN_GROUP, ahead % EXPERT_SLOTS)
            else:
                emit = issuer(idx_ref, t0 + ahead, ahead % EXPERT_SLOTS)
            per = -(-HK // (nlb * (2 if i > 0 else 1)))
            nxt = first_half(i, slot, gates[i:i + 1, :], emit, per)
            if i > 0:
                second_half(i - 1, (i - 1) % EXPERT_SLOTS, act_col, emit, per)
            emit(HK)
            act_col = nxt
        second_half(TOKEN_GROUP - 1, (TOKEN_GROUP - 1) % EXPERT_SLOTS, act_col, lambda n: None, 0)
        o_ref[rows, :] = os_ref[...]

    def body(g, carry):
        group(pl.multiple_of(g * TOKEN_GROUP, TOKEN_GROUP), False)
        return carry

    lax.fori_loop(0, ngroups - 1, body, 0)
    group((ngroups - 1) * TOKEN_GROUP, True)

    @pl.when(step == pl.num_programs(0) - 1)
    def _():
        for t in range(LOOKAHEAD):
            wait(t)


def _peer_experts(idx, gates, xn, x, table):
    n, d = x.shape
    tb = _tile(n, 128)
    assert tb % TOKEN_GROUP == 0 and TOKEN_GROUP % EXPERT_SLOTS == 0 and LOOKAHEAD <= 8
    nrb = n // 8
    next_rows = pl.BlockSpec((8, HK), lambda i: (jnp.minimum((i + 1) * (tb // 8), nrb - 1), 0),
                             memory_space=pltpu.SMEM)
    return pl.pallas_call(
        _peer_experts_kernel,
        grid=(n // tb,),
        in_specs=[
            pl.BlockSpec((tb, HK), lambda i: (i, 0), memory_space=pltpu.SMEM),
            next_rows,
            pl.BlockSpec((tb, HK), lambda i: (i, 0)),
            pl.BlockSpec((tb, d), lambda i: (i, 0)),
            pl.BlockSpec((tb, d), lambda i: (i, 0)),
            pl.BlockSpec(memory_space=pl.ANY),
        ],
        out_specs=pl.BlockSpec((tb, d), lambda i: (i, 0)),
        out_shape=jax.ShapeDtypeStruct((n, d), F32),
        scratch_shapes=[pltpu.VMEM((HK, d), jnp.uint32) for _ in range(EXPERT_SLOTS)]
                       + [pltpu.SemaphoreType.DMA((EXPERT_SLOTS,)),
                          pltpu.VMEM((TOKEN_GROUP, d), F32), pltpu.VMEM((TOKEN_GROUP, d), F32)],
        compiler_params=_cparams("arbitrary"),
        name="peer_experts",
    )(idx, idx, gates, xn, x, table)


def _ple_final_kernel(x_ref, p_ref, gp_ref, wg_ref, bg_ref, wp_ref, gf_ref, o_ref, *, final):
    x = x_ref[...]
    pg = _sigmoid(jnp.dot(_rms(x, gp_ref[...]).astype(BF16), wg_ref[...],
                          preferred_element_type=F32) + bg_ref[...])
    pe = jnp.dot(p_ref[...].astype(BF16), wp_ref[...], preferred_element_type=F32)
    y = x + pe * pg
    o_ref[...] = _rms(y, gf_ref[...]) if final else y


def _ple_final(x, p, g_ple, w_gate, b_gate, w_ple, g_final, final):
    n, d = x.shape
    pd = p.shape[1]
    tm = _tile(n, 256)
    return pl.pallas_call(
        functools.partial(_ple_final_kernel, final=final),
        grid=(n // tm,),
        in_specs=[
            pl.BlockSpec((tm, d), lambda i: (i, 0)),
            pl.BlockSpec((tm, pd), lambda i: (i, 0)),
            _resident((1, d)), _resident(w_gate.shape), _resident((1, d)), _resident(w_ple.shape),
            _resident((1, d)),
        ],
        out_specs=pl.BlockSpec((tm, d), lambda i: (i, 0)),
        out_shape=jax.ShapeDtypeStruct((n, d), F32),
        compiler_params=_cparams("parallel"),
        name="ple_final",
    )(x, p, g_ple.reshape(1, d), w_gate, b_gate.reshape(1, d), w_ple, g_final.reshape(1, d))


def kernel(x_prompt, x_sample, p_prompt, p_sample, norm_mix_g, w_in, w_gate, b_gate, ssm_lam_re, ssm_lam_im, ssm_b_re, ssm_b_im, ssm_c_re, ssm_c_im, ssm_log_step, ssm_d, ssm_w_glu, ssm_b_glu, ssm_w_out, conv_w, conv_b, conv_ln_g, conv_ln_b, conv_w_out, w_o, norm_ffn_g, peer_w_q, peer_keys, peer_u, peer_v, norm_ple_g, ple_w, ple_w_gate, ple_b_gate, norm_final_g):
    depth = w_in.shape[0]
    d = x_prompt.shape[-1]
    seqs = (x_prompt.shape[:2], x_sample.shape[:2])
    n_p = seqs[0][0] * seqs[0][1]
    x = jnp.concatenate([x_prompt.reshape(-1, d), x_sample.reshape(-1, d)], axis=0)
    sw = ssm_w_glu.shape[1]
    cdim = conv_w.shape[2]
    assert sw == cdim and w_in.shape[2] == sw + 2 * cdim and sw % (2 * SSM_GROUP) == 0
    for i in range(depth):
        p = jnp.concatenate([p_prompt[i].reshape(n_p, -1), p_sample[i].reshape(x.shape[0] - n_p, -1)], axis=0)
        w_in_b = w_in[i].astype(BF16)
        zs = _norm_matmul(x, norm_mix_g[i], w_in_b[:, :sw], jnp.zeros((sw,), F32), "none", F32)
        zc = _norm_matmul(x, norm_mix_g[i], w_in_b[:, sw:], jnp.zeros((2 * cdim,), F32), "none", BF16)
        sg = _norm_matmul(x, norm_mix_g[i], w_gate[i].astype(BF16), b_gate[i], "sigmoid", BF16)
        y = _s5_branch(zs, seqs,
                       (ssm_lam_re[i], ssm_lam_im[i], ssm_b_re[i], ssm_b_im[i], ssm_c_re[i],
                        ssm_c_im[i], ssm_log_step[i], ssm_d[i]))
        cb = _conv_branch(zc, 0, cdim, seqs, conv_w[i], conv_b[i], conv_ln_g[i], conv_ln_b[i])
        x = _mixer_out(y, cb, sg, x, ssm_w_glu[i].astype(BF16), ssm_b_glu[i], ssm_w_out[i].astype(BF16),
                       conv_w_out[i].astype(BF16), w_o[i].astype(BF16))
        xn, idx, gates = _peer_route(x, norm_ffn_g[i], peer_w_q[i].astype(BF16), peer_keys[i].astype(BF16))
        x = _peer_experts(idx, gates, xn, x, _pack_expert_tables(peer_u[i], peer_v[i]))
        x = _ple_final(x, p, norm_ple_g[i], ple_w_gate[i].astype(BF16), ple_b_gate[i],
                       ple_w[i].astype(BF16), norm_final_g, i == depth - 1)
    y_prompt = x[:n_p].reshape(x_prompt.shape)
    y_sample = x[n_p:].reshape(x_sample.shape)
    return (y_prompt, y_sample)
```

```python
import functools
import math

import jax
import jax.numpy as jnp
from jax import lax
from jax.experimental import pallas as pl
from jax.experimental.pallas import tpu as pltpu

F32 = jnp.float32
BF16 = jnp.bfloat16
I32 = jnp.int32

EPS = 1e-6
LANES = 128
SSM_GROUP = 16
SSM_STATE = 64
S5_CHUNK = 16
CONV_WIDTH = 31
CONV_PAD = CONV_WIDTH // 2
HALO = 16
PEER_HEADS = 8
PEER_HALF = 128
N_KEYS = 128
PEER_TOPK = 16
HK = PEER_HEADS * PEER_TOPK
CAND_PAIRS = tuple((i, j) for i in range(PEER_TOPK) for j in range(PEER_TOPK // (i + 1)))
N_CAND = -(-len(CAND_PAIRS) // 8) * 8
VMEM_LIMIT = 56 * 1024 * 1024


def _cparams(*sem):
    return pltpu.CompilerParams(dimension_semantics=sem, vmem_limit_bytes=VMEM_LIMIT)


def _tile(n, pref):
    t = min(n, pref)
    while n % t:
        t //= 2
    return t


def _sigmoid(x):
    return 1.0 / (1.0 + jnp.exp(-x))


def _gelu(x):
    return 0.5 * x * (1.0 + lax.erf(x * (1.0 / math.sqrt(2.0))))


def _rms(x, g):
    return x * lax.rsqrt(jnp.mean(x * x, axis=-1, keepdims=True) + EPS) * g


def _resident(shape):
    nd = len(shape)
    return pl.BlockSpec(shape, lambda *_: (0,) * nd, pipeline_mode=pl.Buffered(1))


def _norm_matmul_kernel(x_ref, g_ref, w_ref, b_ref, o_ref, h_ref, *, act):
    @pl.when(pl.program_id(1) == 0)
    def _():
        h_ref[...] = _rms(x_ref[...], g_ref[...]).astype(BF16)

    y = jnp.dot(h_ref[...], w_ref[...], preferred_element_type=F32) + b_ref[...]
    if act == "sigmoid":
        y = _sigmoid(y)
    o_ref[...] = y.astype(o_ref.dtype)


def _norm_matmul(x, g, w, b, act, out_dtype):
    n, d = x.shape
    cols = w.shape[1]
    tm = _tile(n, 512)
    tn = _tile(cols, 1024)
    return pl.pallas_call(
        functools.partial(_norm_matmul_kernel, act=act),
        grid=(n // tm, cols // tn),
        in_specs=[
            pl.BlockSpec((tm, d), lambda i, j: (i, 0)),
            pl.BlockSpec((1, d), lambda i, j: (0, 0)),
            pl.BlockSpec((d, tn), lambda i, j: (0, j)),
            pl.BlockSpec((1, tn), lambda i, j: (0, j)),
        ],
        out_specs=pl.BlockSpec((tm, tn), lambda i, j: (i, j)),
        out_shape=jax.ShapeDtypeStruct((n, cols), out_dtype),
        scratch_shapes=[pltpu.VMEM((tm, d), BF16)],
        compiler_params=_cparams("parallel", "arbitrary"),
        name="norm_matmul",
    )(x, g.reshape(1, d), w, b.reshape(1, cols))


def _s5_weights(lam_re, lam_im, b_re, b_im, c_re, c_im, log_step, d_skip):
    t = S5_CHUNK
    g = lam_re.shape[1]
    step = jnp.exp(log_step)[..., None]
    zr, zi = lam_re * step, lam_im * step
    dd = jnp.arange(t + 1, dtype=F32).reshape(t + 1, 1, 1, 1)
    mag = jnp.exp(dd * zr)
    pw_r, pw_i = mag * jnp.cos(dd * zi), mag * jnp.sin(dd * zi)
    nr, ni = pw_r[1] - 1.0, pw_i[1]
    den = lam_re * lam_re + lam_im * lam_im
    qr = (nr * lam_re + ni * lam_im) / den
    qi = (ni * lam_re - nr * lam_im) / den
    bb_r = qr[..., None] * b_re - qi[..., None] * b_im
    bb_i = qr[..., None] * b_im + qi[..., None] * b_re
    lb_r = pw_r[:t, ..., None] * bb_r - pw_i[:t, ..., None] * bb_i
    lb_i = pw_r[:t, ..., None] * bb_i + pw_i[:t, ..., None] * bb_r
    hi = lax.Precision.HIGHEST
    kern = (jnp.einsum("xgkp,dxgph->xgdkh", c_re, lb_r, precision=hi)
            - jnp.einsum("xgkp,dxgph->xgdkh", c_im, lb_i, precision=hi))
    gb = LANES // SSM_GROUP
    nb = g // gb
    hh, pp = SSM_GROUP, SSM_STATE

    def diag_tiles(a):
        lead, (r, w) = a.shape[1:-2], a.shape[-2:]
        a = jnp.moveaxis(a.reshape((nb, gb) + lead + (r, w)), 1, -3)
        a = jnp.tile(a, (1,) * (a.ndim - 1) + (gb,))
        keep = jnp.arange(gb)[:, None, None] == (jnp.arange(gb * w) // w)[None, None, :]
        return jnp.where(keep, a, 0.0).reshape((nb,) + lead + (gb * r, gb * w))

    rev = (t - 1) - jnp.arange(t)
    kf, kb = jnp.swapaxes(kern[0], -1, -2), jnp.swapaxes(kern[1], -1, -2)
    lag0 = kf[:, 0] + kb[:, 0] + d_skip.reshape(g, hh, 1) * jnp.eye(hh, dtype=F32)
    lags = jnp.concatenate([kb[:, rev[:t - 1]], lag0[:, None], kf[:, 1:]], axis=1)
    tiles = diag_tiles(lags).astype(BF16)
    lag_idx = jnp.arange(t)[None, :] - jnp.arange(t)[:, None] + (t - 1)
    w1y = jnp.transpose(tiles[:, lag_idx], (0, 1, 3, 2, 4)).reshape(nb, t * LANES, t * LANES)
    ends = [jnp.transpose(e, (1, 0, 3, 2)) for e in (lb_r[rev, 0], lb_i[rev, 0], lb_r[:, 1], lb_i[:, 1])]
    w1 = jnp.concatenate([w1y] + [diag_tiles(e).astype(BF16).reshape(nb, t * LANES, gb * pp)
                                  for e in ends], axis=2)
    pf_r, pf_i = pw_r[1:, 0], pw_i[1:, 0]
    pb_r, pb_i = pw_r[rev + 1, 1], pw_i[rev + 1, 1]

    def carry_rows(cr, ci, pr, pi):
        re = cr[None] * pr[:, :, None, :] - ci[None] * pi[:, :, None, :]
        im = cr[None] * pi[:, :, None, :] + ci[None] * pr[:, :, None, :]
        return jnp.transpose(re, (1, 0, 3, 2)), -jnp.transpose(im, (1, 0, 3, 2))

    carries = carry_rows(c_re[0], c_im[0], pf_r, pf_i) + carry_rows(c_re[1], c_im[1], pb_r, pb_i)
    w2 = jnp.concatenate([jnp.transpose(diag_tiles(c).astype(BF16), (0, 2, 1, 3))
                          .reshape(nb, gb * pp, t * LANES) for c in carries], axis=1)
    zeros = jnp.zeros((nb, gb * pp), F32)
    mult = jnp.stack([pw_r[t, 0].reshape(nb, -1), pw_i[t, 0].reshape(nb, -1),
                      pw_r[t, 1].reshape(nb, -1), pw_i[t, 1].reshape(nb, -1),
                      zeros, zeros, zeros, zeros], axis=1)
    return w1.astype(BF16), w2.astype(BF16), mult


def _s5_kernel(u_ref, w1_ref, w2_ref, a_ref, o_ref, r_ref, sp_ref, *, variants):
    t = S5_CHUNK
    nc = u_ref.shape[0] // t
    ny = t * LANES
    ns = sp_ref.shape[1] // 4
    x = jnp.concatenate([u_ref[pl.ds(tau, nc, stride=t), :].astype(BF16) for tau in range(t)], axis=1)
    r_ref[...] = jnp.dot(x, w1_ref[0], preferred_element_type=F32)
    a = a_ref[0]
    afr, afi, abr, abi = a[0:1], a[1:2], a[2:3], a[3:4]

    def scan(nseq, nch):
        zero = jnp.zeros((1, ns), F32)
        state = [(zero, zero, zero, zero)] * nseq
        for c in range(nch):
            for s in range(nseq):
                fr, fi, br, bi = state[s]
                rf = slice(s * nch + c, s * nch + c + 1)
                rb = slice(s * nch + nch - 1 - c, s * nch + nch - c)
                sp_ref[rf, 0:ns] = fr
                sp_ref[rf, ns:2 * ns] = fi
                sp_ref[rb, 2 * ns:3 * ns] = br
                sp_ref[rb, 3 * ns:4 * ns] = bi
                er, ei = r_ref[rf, ny:ny + ns], r_ref[rf, ny + ns:ny + 2 * ns]
                gr, gi = r_ref[rb, ny + 2 * ns:ny + 3 * ns], r_ref[rb, ny + 3 * ns:ny + 4 * ns]
                state[s] = (afr * fr - afi * fi + er, afr * fi + afi * fr + ei,
                            abr * br - abi * bi + gr, abr * bi + abi * br + gi)

    if len(variants) == 1:
        scan(*variants[0][2:])
    else:
        blk = pl.program_id(1)
        for lo, hi, nseq, nch in variants:
            pl.when((blk >= lo) & (blk < hi))(functools.partial(scan, nseq, nch))
    ytot = r_ref[:, 0:ny] + jnp.dot(sp_ref[...].astype(BF16), w2_ref[0], preferred_element_type=F32)
    for tt in range(t):
        o_ref[pl.ds(tt, nc, stride=t), :] = ytot[:, tt * LANES:(tt + 1) * LANES]


def _s5_branch(u, seqs, params):
    n, width = u.shape
    t = S5_CHUNK
    w1, w2, mult = _s5_weights(*params)
    nb = width // LANES
    rblk = max(s for _, s in seqs)
    variants, blk = [], 0
    for bsz, seqlen in seqs:
        assert rblk % seqlen == 0 and (bsz * seqlen) % rblk == 0 and seqlen % t == 0
        nblk = bsz * seqlen // rblk
        variants.append((blk, blk + nblk, rblk // seqlen, seqlen // t))
        blk += nblk
    if all(v[2:] == variants[0][2:] for v in variants):
        variants = [(0, blk) + variants[0][2:]]
    nc = rblk // t
    nst = w1.shape[2] - t * LANES
    return pl.pallas_call(
        functools.partial(_s5_kernel, variants=tuple(variants)),
        grid=(nb, n // rblk),
        in_specs=[
            pl.BlockSpec((rblk, LANES), lambda b, r: (r, b)),
            pl.BlockSpec((1,) + w1.shape[1:], lambda b, r: (b, 0, 0), pipeline_mode=pl.Buffered(1)),
            pl.BlockSpec((1,) + w2.shape[1:], lambda b, r: (b, 0, 0), pipeline_mode=pl.Buffered(1)),
            pl.BlockSpec((1,) + mult.shape[1:], lambda b, r: (b, 0, 0)),
        ],
        out_specs=pl.BlockSpec((rblk, LANES), lambda b, r: (r, b)),
        out_shape=jax.ShapeDtypeStruct((n, width), F32),
        scratch_shapes=[pltpu.VMEM((nc, w1.shape[2]), F32), pltpu.VMEM((nc, nst), F32)],
        compiler_params=_cparams("parallel", "parallel"),
        name="s5_chunks",
    )(u, w1, w2, mult)


def _conv_kernel(vp_ref, gp_ref, vc_ref, gc_ref, vn_ref, gn_ref, w_ref, cb_ref, lg_ref, lb_ref,
                 o_ref, ext_ref, acc_ref, *, tl, bounds):
    r0 = pl.program_id(0) * tl
    pos, length = r0, bounds[0][1]
    for start, seqlen in bounds:
        inside = r0 >= start
        pos = jnp.where(inside, lax.rem(r0 - start, seqlen), pos)
        length = jnp.where(inside, seqlen, length)
    left_ok = (pos != 0).astype(F32)
    right_ok = (pos + tl != length).astype(F32)

    def glu(v_ref, g_ref):
        return v_ref[...].astype(F32) * _sigmoid(g_ref[...].astype(F32))

    ext_ref[0:HALO, :] = glu(vp_ref, gp_ref) * left_ok
    ext_ref[HALO:HALO + tl, :] = glu(vc_ref, gc_ref)
    ext_ref[HALO + tl:2 * HALO + tl, :] = glu(vn_ref, gn_ref) * right_ok

    cdim = o_ref.shape[1]
    rb = 64
    off = HALO - CONV_PAD

    def rows(i, carry):
        base = pl.multiple_of(i * rb, rb)
        for lb in range(cdim // LANES):
            lanes = slice(lb * LANES, (lb + 1) * LANES)
            win = ext_ref[pl.ds(base, rb + 2 * HALO), lanes]
            acc = jnp.zeros((rb, LANES), F32) + cb_ref[:, lanes]
            for k in range(CONV_WIDTH):
                acc = acc + w_ref[k:k + 1, lanes] * win[off + k:off + k + rb, :]
            acc_ref[pl.ds(base, rb), lanes] = acc
        return carry

    lax.fori_loop(0, tl // rb, rows, 0)
    hf = acc_ref[...]
    mu = jnp.mean(hf, axis=-1, keepdims=True)
    cen = hf - mu
    var = jnp.mean(cen * cen, axis=-1, keepdims=True)
    hn = cen * lax.rsqrt(var + EPS) * lg_ref[...] + lb_ref[...]
    o_ref[...] = (hn * _sigmoid(hn)).astype(o_ref.dtype)


def _conv_branch(z, col0, cdim, seqs, conv_w, conv_b, ln_g, ln_b):
    n = z.shape[0]
    tl = _tile(min(s for _, s in seqs), 256)
    hb = tl // HALO
    nhb = n // HALO
    bounds, tok = [], 0
    for bsz, seqlen in seqs:
        bounds.append((tok, seqlen))
        tok += bsz * seqlen
    cur = lambda c: pl.BlockSpec((tl, cdim), lambda i: (i, c))
    prev = lambda c: pl.BlockSpec((HALO, cdim), lambda i: (jnp.maximum(i * hb - 1, 0), c))
    nxt = lambda c: pl.BlockSpec((HALO, cdim), lambda i: (jnp.minimum((i + 1) * hb, nhb - 1), c))
    vec = pl.BlockSpec((1, cdim), lambda i: (0, 0))
    return pl.pallas_call(
        functools.partial(_conv_kernel, tl=tl, bounds=tuple(bounds)),
        grid=(n // tl,),
        in_specs=[prev(col0), prev(col0 + 1), cur(col0), cur(col0 + 1), nxt(col0), nxt(col0 + 1),
                  pl.BlockSpec((CONV_WIDTH, cdim), lambda i: (0, 0)), vec, vec, vec],
        out_specs=pl.BlockSpec((tl, cdim), lambda i: (i, 0)),
        out_shape=jax.ShapeDtypeStruct((n, cdim), BF16),
        scratch_shapes=[pltpu.VMEM((tl + 2 * HALO, cdim), F32), pltpu.VMEM((tl, cdim), F32)],
        compiler_params=_cparams("parallel"),
        name="conv_branch",
    )(z, z, z, z, z, z, conv_w, conv_b.reshape(1, cdim), ln_g.reshape(1, cdim), ln_b.reshape(1, cdim))


def _mixer_out_kernel(y_ref, cb_ref, ga_ref, gb_ref, x_ref, wglu_ref, bglu_ref, wa_ref, wb_ref,
                      wo_ref, o_ref):
    hg = _gelu(y_ref[...])
    gate = _sigmoid(jnp.dot(hg.astype(BF16), wglu_ref[...], preferred_element_type=F32)
                    + bglu_ref[...])
    a = jnp.dot((hg * gate).astype(BF16), wa_ref[...], preferred_element_type=F32)
    b = jnp.dot(cb_ref[...], wb_ref[...], preferred_element_type=F32)
    merged = ga_ref[...].astype(F32) * a + gb_ref[...].astype(F32) * b
    o_ref[...] = x_ref[...] + jnp.dot(merged.astype(BF16), wo_ref[...], preferred_element_type=F32)


def _mixer_out(y, cb, sg, x, w_glu, b_glu, w_a, w_b, w_o):
    n, d = x.shape
    sw = y.shape[1]
    cd = cb.shape[1]
    tm = _tile(n, 256)
    return pl.pallas_call(
        _mixer_out_kernel,
        grid=(n // tm,),
        in_specs=[
            pl.BlockSpec((tm, sw), lambda i: (i, 0)),
            pl.BlockSpec((tm, cd), lambda i: (i, 0)),
            pl.BlockSpec((tm, d), lambda i: (i, 0)),
            pl.BlockSpec((tm, d), lambda i: (i, 1)),
            pl.BlockSpec((tm, d), lambda i: (i, 0)),
            _resident(w_glu.shape), _resident((1, sw)), _resident(w_a.shape), _resident(w_b.shape),
            _resident(w_o.shape),
        ],
        out_specs=pl.BlockSpec((tm, d), lambda i: (i, 0)),
        out_shape=jax.ShapeDtypeStruct((n, d), F32),
        compiler_params=_cparams("parallel"),
        name="mixer_out",
    )(y, cb, sg, sg, x, w_glu, b_glu.reshape(1, sw), w_a, w_b, w_o)


def _top_rows(s, iota, count):
    nrows = s.shape[0]
    vals, idxs = [], []
    for _ in range(count):
        m = jnp.max(s, axis=0, keepdims=True)
        idx = jnp.min(jnp.where(s == m, iota, float(nrows)), axis=0, keepdims=True)
        vals.append(m)
        idxs.append(idx)
        s = jnp.where(iota == idx, -jnp.inf, s)
    return vals, idxs


def _peer_route_kernel(x_ref, g_ref, wq_ref, keys_ref, xn_ref, idx_ref, gate_ref, it_ref, gt_ref,
                       cs_ref, ce_ref):
    tm = x_ref.shape[0]
    xn = _rms(x_ref[...], g_ref[...])
    xn_ref[...] = xn
    q = jnp.dot(xn.astype(BF16), wq_ref[...], preferred_element_type=F32).astype(BF16)
    iota_k = lax.broadcasted_iota(I32, (N_KEYS, tm), 0).astype(F32)
    pairs = CAND_PAIRS
    ncand = N_CAND
    iota_c = lax.broadcasted_iota(I32, (ncand, tm), 0).astype(F32)
    for h in range(PEER_HEADS):
        tops = []
        for c in range(2):
            qc = q[:, (2 * h + c) * PEER_HALF:(2 * h + c + 1) * PEER_HALF]
            s = lax.dot_general(keys_ref[h, c], qc, (((1,), (1,)), ((), ())),
                                preferred_element_type=F32)
            tops.append(_top_rows(s, iota_k, PEER_TOPK))
        (s1, i1), (s2, i2) = tops
        cs_ref[...] = jnp.full((ncand, tm), -jnp.inf, F32)
        ce_ref[...] = jnp.zeros((ncand, tm), F32)
        for r, (i, j) in enumerate(pairs):
            cs_ref[r:r + 1, :] = s1[i] + s2[j]
            ce_ref[r:r + 1, :] = i1[i] * float(N_KEYS) + i2[j]
        cs = cs_ref[...]
        ce = ce_ref[...]
        best = []
        for k in range(PEER_TOPK):
            m = jnp.max(cs, axis=0, keepdims=True)
            pos = jnp.min(jnp.where(cs == m, iota_c, float(ncand)), axis=0, keepdims=True)
            hit = iota_c == pos
            it_ref[h * PEER_TOPK + k:h * PEER_TOPK + k + 1, :] = jnp.max(
                jnp.where(hit, ce, -1.0), axis=0, keepdims=True)
            best.append(m)
            cs = jnp.where(hit, -jnp.inf, cs)
        ex = [jnp.exp(b - best[0]) for b in best]
        inv = 1.0 / functools.reduce(lambda p, r: p + r, ex)
        for k in range(PEER_TOPK):
            gt_ref[h * PEER_TOPK + k:h * PEER_TOPK + k + 1, :] = ex[k] * inv
    idx_ref[...] = it_ref[...].T.astype(I32)
    gate_ref[...] = gt_ref[...].T


def _peer_route(x, g, w_q, keys):
    n, d = x.shape
    tm = _tile(n, 256)
    qcols = w_q.shape[1]
    return pl.pallas_call(
        _peer_route_kernel,
        grid=(n // tm,),
        in_specs=[
            pl.BlockSpec((tm, d), lambda i: (i, 0)),
            _resident((1, d)), _resident((d, qcols)), _resident(keys.shape),
        ],
        out_specs=[
            pl.BlockSpec((tm, d), lambda i: (i, 0)),
            pl.BlockSpec((tm, HK), lambda i: (i, 0)),
            pl.BlockSpec((tm, HK), lambda i: (i, 0)),
        ],
        out_shape=[
            jax.ShapeDtypeStruct((n, d), F32),
            jax.ShapeDtypeStruct((n, HK), I32),
            jax.ShapeDtypeStruct((n, HK), F32),
        ],
        scratch_shapes=[pltpu.VMEM((HK, tm), F32), pltpu.VMEM((HK, tm), F32),
                        pltpu.VMEM((N_CAND, tm), F32), pltpu.VMEM((N_CAND, tm), F32)],
        compiler_params=_cparams("parallel"),
        name="peer_route",
    )(x, g.reshape(1, d), w_q, keys)


EXPERT_SLOTS = 8
LOOKAHEAD = EXPERT_SLOTS - 2
TOKEN_GROUP = 16


def _pack_tables_kernel(u_ref, v_ref, o_ref):
    packed = pltpu.pack_elementwise([v_ref[...], u_ref[...]], packed_dtype=BF16)
    for r in range(packed.shape[0]):
        o_ref[r] = packed[r:r + 1, :]


def _pack_expert_tables(u_tab, v_tab):
    e, d = u_tab.shape
    te = _tile(e, 64)
    return pl.pallas_call(
        _pack_tables_kernel,
        grid=(e // te,),
        in_specs=[pl.BlockSpec((te, d), lambda i: (i, 0)), pl.BlockSpec((te, d), lambda i: (i, 0))],
        out_specs=pl.BlockSpec((te, 1, d), lambda i: (i, 0, 0)),
        out_shape=jax.ShapeDtypeStruct((e, 1, d), jnp.uint32),
        compiler_params=_cparams("parallel"),
        name="pack_tables",
    )(u_tab, v_tab)


def _peer_experts_kernel(idx_ref, idx_next_ref, gate_ref, xn_ref, x_ref, tab_ref, o_ref, *scratch):
    bufs = scratch[:EXPERT_SLOTS]
    sem_ref, xs_ref, os_ref = scratch[EXPERT_SLOTS:]
    tb, d = xn_ref.shape
    nlb = d // LANES
    ngroups = tb // TOKEN_GROUP
    step = pl.program_id(0)

    def issuer(ref, row, slot):
        ks = iter(range(HK))

        def emit(n):
            for k in [k for _, k in zip(range(n), ks)]:
                pltpu.make_async_copy(tab_ref.at[ref[row, k]],
                                      bufs[slot].at[pl.ds(k, 1), :], sem_ref.at[slot]
                                      ).start(priority=k % 2)
        return emit

    def issue(ref, row, slot):
        issuer(ref, row, slot)(HK)

    def wait(slot):
        pltpu.make_async_copy(bufs[slot], bufs[slot], sem_ref.at[slot]).wait()

    @pl.when(step == 0)
    def _():
        for t in range(LOOKAHEAD):
            issue(idx_ref, t, t)

    def first_half(i, slot, gate, emit, per):
        acc = jnp.zeros((HK, LANES), F32)
        for j in range(nlb):
            lanes = slice(j * LANES, (j + 1) * LANES)
            u = pltpu.unpack_elementwise(bufs[slot][:, lanes], index=1, packed_dtype=BF16,
                                         unpacked_dtype=F32)
            acc = acc + u * xs_ref[i:i + 1, lanes]
            emit(per)
        s = jnp.sum(acc.T, axis=0, keepdims=True)
        act = _gelu(s) * gate
        return jnp.broadcast_to(act, (LANES, HK)).T

    def second_half(i, slot, act_col, emit, per):
        for j in range(nlb):
            lanes = slice(j * LANES, (j + 1) * LANES)
            v = pltpu.unpack_elementwise(bufs[slot][:, lanes], index=0, packed_dtype=BF16,
                                         unpacked_dtype=F32)
            os_ref[i:i + 1, lanes] += jnp.sum(v * act_col, axis=0, keepdims=True)
            emit(per)

    def group(t0, last):
        rows = pl.ds(t0, TOKEN_GROUP)
        xs_ref[...] = xn_ref[rows, :]
        os_ref[...] = x_ref[rows, :]
        gates = gate_ref[rows, :]
        act_col = None
        for i in range(TOKEN_GROUP):
            slot = i % EXPERT_SLOTS
            wait(slot)
            ahead = i + LOOKAHEAD
            if last and ahead >= TOKEN_GROUP:
                emit = issuer(idx_next_ref, ahead - TOKEN_GROUP, ahead % EXPERT_SLOTS)
            else:
                emit = issuer(idx_ref, t0 + ahead, ahead % EXPERT_SLOTS)
            per = -(-HK // (nlb * (2 if i > 0 else 1)))
            nxt = first_half(i, slot, gates[i:i + 1, :], emit, per)
            if i > 0:
                second_half(i - 1, (i - 1) % EXPERT_SLOTS, act_col, emit, per)
            emit(HK)
            act_col = nxt
        second_half(TOKEN_GROUP - 1, (TOKEN_GROUP - 1) % EXPERT_SLOTS, act_col, lambda n: None, 0)
        o_ref[rows, :] = os_ref[...]

    def body(g, carry):
        group(pl.multiple_of(g * TOKEN_GROUP, TOKEN_GROUP), False)
        return carry

    lax.fori_loop(0, ngroups - 1, body, 0)
    group((ngroups - 1) * TOKEN_GROUP, True)

    @pl.when(step == pl.num_programs(0) - 1)
    def _():
        for t in range(LOOKAHEAD):
            wait(t)


def _peer_experts(idx, gates, xn, x, table):
    n, d = x.shape
    tb = _tile(n, 128)
    assert tb % TOKEN_GROUP == 0 and TOKEN_GROUP % EXPERT_SLOTS == 0 and LOOKAHEAD <= 8
    nrb = n // 8
    next_rows = pl.BlockSpec((8, HK), lambda i: (jnp.minimum((i + 1) * (tb // 8), nrb - 1), 0),
                             memory_space=pltpu.SMEM)
    return pl.pallas_call(
        _peer_experts_kernel,
        grid=(n // tb,),
        in_specs=[
            pl.BlockSpec((tb, HK), lambda i: (i, 0), memory_space=pltpu.SMEM),
            next_rows,
            pl.BlockSpec((tb, HK), lambda i: (i, 0)),
            pl.BlockSpec((tb, d), lambda i: (i, 0)),
            pl.BlockSpec((tb, d), lambda i: (i, 0)),
            pl.BlockSpec(memory_space=pl.ANY),
        ],
        out_specs=pl.BlockSpec((tb, d), lambda i: (i, 0)),
        out_shape=jax.ShapeDtypeStruct((n, d), F32),
        scratch_shapes=[pltpu.VMEM((HK, d), jnp.uint32) for _ in range(EXPERT_SLOTS)]
                       + [pltpu.SemaphoreType.DMA((EXPERT_SLOTS,)),
                          pltpu.VMEM((TOKEN_GROUP, d), F32), pltpu.VMEM((TOKEN_GROUP, d), F32)],
        compiler_params=_cparams("arbitrary"),
        name="peer_experts",
    )(idx, idx, gates, xn, x, table)


def _ple_final_kernel(x_ref, p_ref, gp_ref, wg_ref, bg_ref, wp_ref, gf_ref, o_ref, *, final):
    x = x_ref[...]
    pg = _sigmoid(jnp.dot(_rms(x, gp_ref[...]).astype(BF16), wg_ref[...],
                          preferred_element_type=F32) + bg_ref[...])
    pe = jnp.dot(p_ref[...].astype(BF16), wp_ref[...], preferred_element_type=F32)
    y = x + pe * pg
    o_ref[...] = _rms(y, gf_ref[...]) if final else y


def _ple_final(x, p, g_ple, w_gate, b_gate, w_ple, g_final, final):
    n, d = x.shape
    pd = p.shape[1]
    tm = _tile(n, 256)
    return pl.pallas_call(
        functools.partial(_ple_final_kernel, final=final),
        grid=(n // tm,),
        in_specs=[
            pl.BlockSpec((tm, d), lambda i: (i, 0)),
            pl.BlockSpec((tm, pd), lambda i: (i, 0)),
            _resident((1, d)), _resident(w_gate.shape), _resident((1, d)), _resident(w_ple.shape),
            _resident((1, d)),
        ],
        out_specs=pl.BlockSpec((tm, d), lambda i: (i, 0)),
        out_shape=jax.ShapeDtypeStruct((n, d), F32),
        compiler_params=_cparams("parallel"),
        name="ple_final",
    )(x, p, g_ple.reshape(1, d), w_gate, b_gate.reshape(1, d), w_ple, g_final.reshape(1, d))


def kernel(x_prompt, x_sample, p_prompt, p_sample, norm_mix_g, w_in, w_gate, b_gate, ssm_lam_re, ssm_lam_im, ssm_b_re, ssm_b_im, ssm_c_re, ssm_c_im, ssm_log_step, ssm_d, ssm_w_glu, ssm_b_glu, ssm_w_out, conv_w, conv_b, conv_ln_g, conv_ln_b, conv_w_out, w_o, norm_ffn_g, peer_w_q, peer_keys, peer_u, peer_v, norm_ple_g, ple_w, ple_w_gate, ple_b_gate, norm_final_g):
    depth = w_in.shape[0]
    d = x_prompt.shape[-1]
    seqs = (x_prompt.shape[:2], x_sample.shape[:2])
    n_p = seqs[0][0] * seqs[0][1]
    x = jnp.concatenate([x_prompt.reshape(-1, d), x_sample.reshape(-1, d)], axis=0)
    sw = ssm_w_glu.shape[1]
    cdim = conv_w.shape[2]
    assert sw == cdim and w_in.shape[2] == sw + 2 * cdim and sw % (2 * SSM_GROUP) == 0
    for i in range(depth):
        p = jnp.concatenate([p_prompt[i].reshape(n_p, -1), p_sample[i].reshape(x.shape[0] - n_p, -1)], axis=0)
        w_in_b = w_in[i].astype(BF16)
        zs = _norm_matmul(x, norm_mix_g[i], w_in_b[:, :sw], jnp.zeros((sw,), F32), "none", F32)
        zc = _norm_matmul(x, norm_mix_g[i], w_in_b[:, sw:], jnp.zeros((2 * cdim,), F32), "none", BF16)
        sg = _norm_matmul(x, norm_mix_g[i], w_gate[i].astype(BF16), b_gate[i], "sigmoid", BF16)
        y = _s5_branch(zs, seqs,
                       (ssm_lam_re[i], ssm_lam_im[i], ssm_b_re[i], ssm_b_im[i], ssm_c_re[i],
                        ssm_c_im[i], ssm_log_step[i], ssm_d[i]))
        cb = _conv_branch(zc, 0, cdim, seqs, conv_w[i], conv_b[i], conv_ln_g[i], conv_ln_b[i])
        x = _mixer_out(y, cb, sg, x, ssm_w_glu[i].astype(BF16), ssm_b_glu[i], ssm_w_out[i].astype(BF16),
                       conv_w_out[i].astype(BF16), w_o[i].astype(BF16))
        xn, idx, gates = _peer_route(x, norm_ffn_g[i], peer_w_q[i].astype(BF16), peer_keys[i].astype(BF16))
        x = _peer_experts(idx, gates, xn, x, _pack_expert_tables(peer_u[i], peer_v[i]))
        x = _ple_final(x, p, norm_ple_g[i], ple_w_gate[i].astype(BF16), ple_b_gate[i],
                       ple_w[i].astype(BF16), norm_final_g, i == depth - 1)
    y_prompt = x[:n_p].reshape(x_prompt.shape)
    y_sample = x[n_p:].reshape(x_sample.shape)
    return (y_prompt, y_sample)
```

```python
import functools
import math

import jax
import jax.numpy as jnp
from jax import lax
from jax.experimental import pallas as pl
from jax.experimental.pallas import tpu as pltpu

F32 = jnp.float32
BF16 = jnp.bfloat16
I32 = jnp.int32

EPS = 1e-6
LANES = 128
SSM_GROUP = 16
SSM_STATE = 64
S5_CHUNK = 16
CONV_WIDTH = 31
CONV_PAD = CONV_WIDTH // 2
HALO = 16
PEER_HEADS = 8
PEER_HALF = 128
N_KEYS = 128
PEER_TOPK = 16
HK = PEER_HEADS * PEER_TOPK
CAND_PAIRS = tuple((i, j) for i in range(PEER_TOPK) for j in range(PEER_TOPK // (i + 1)))
N_CAND = -(-len(CAND_PAIRS) // 8) * 8
VMEM_LIMIT = 56 * 1024 * 1024


def _cparams(*sem):
    return pltpu.CompilerParams(dimension_semantics=sem, vmem_limit_bytes=VMEM_LIMIT)


def _tile(n, pref):
    t = min(n, pref)
    while n % t:
        t //= 2
    return t


def _sigmoid(x):
    return 1.0 / (1.0 + jnp.exp(-x))


def _gelu(x):
    return 0.5 * x * (1.0 + lax.erf(x * (1.0 / math.sqrt(2.0))))


def _rms(x, g):
    return x * lax.rsqrt(jnp.mean(x * x, axis=-1, keepdims=True) + EPS) * g


def _resident(shape):
    nd = len(shape)
    return pl.BlockSpec(shape, lambda *_: (0,) * nd, pipeline_mode=pl.Buffered(1))


def _norm_matmul_kernel(x_ref, g_ref, w_ref, b_ref, o_ref, h_ref, *, act):
    @pl.when(pl.program_id(1) == 0)
    def _():
        h_ref[...] = _rms(x_ref[...], g_ref[...]).astype(BF16)

    y = jnp.dot(h_ref[...], w_ref[...], preferred_element_type=F32) + b_ref[...]
    if act == "sigmoid":
        y = _sigmoid(y)
    o_ref[...] = y.astype(o_ref.dtype)


def _norm_matmul(x, g, w, b, act, out_dtype):
    n, d = x.shape
    cols = w.shape[1]
    tm = _tile(n, 512)
    tn = _tile(cols, 1024)
    return pl.pallas_call(
        functools.partial(_norm_matmul_kernel, act=act),
        grid=(n // tm, cols // tn),
        in_specs=[
            pl.BlockSpec((tm, d), lambda i, j: (i, 0)),
            pl.BlockSpec((1, d), lambda i, j: (0, 0)),
            pl.BlockSpec((d, tn), lambda i, j: (0, j)),
            pl.BlockSpec((1, tn), lambda i, j: (0, j)),
        ],
        out_specs=pl.BlockSpec((tm, tn), lambda i, j: (i, j)),
        out_shape=jax.ShapeDtypeStruct((n, cols), out_dtype),
        scratch_shapes=[pltpu.VMEM((tm, d), BF16)],
        compiler_params=_cparams("parallel", "arbitrary"),
        name="norm_matmul",
    )(x, g.reshape(1, d), w, b.reshape(1, cols))


def _s5_weights(lam_re, lam_im, b_re, b_im, c_re, c_im, log_step, d_skip):
    t = S5_CHUNK
    g = lam_re.shape[1]
    step = jnp.exp(log_step)[..., None]
    zr, zi = lam_re * step, lam_im * step
    dd = jnp.arange(t + 1, dtype=F32).reshape(t + 1, 1, 1, 1)
    mag = jnp.exp(dd * zr)
    pw_r, pw_i = mag * jnp.cos(dd * zi), mag * jnp.sin(dd * zi)
    nr, ni = pw_r[1] - 1.0, pw_i[1]
    den = lam_re * lam_re + lam_im * lam_im
    qr = (nr * lam_re + ni * lam_im) / den
    qi = (ni * lam_re - nr * lam_im) / den
    bb_r = qr[..., None] * b_re - qi[..., None] * b_im
    bb_i = qr[..., None] * b_im + qi[..., None] * b_re
    lb_r = pw_r[:t, ..., None] * bb_r - pw_i[:t, ..., None] * bb_i
    lb_i = pw_r[:t, ..., None] * bb_i + pw_i[:t, ..., None] * bb_r
    hi = lax.Precision.HIGHEST
    kern = (jnp.einsum("xgkp,dxgph->xgdkh", c_re, lb_r, precision=hi)
            - jnp.einsum("xgkp,dxgph->xgdkh", c_im, lb_i, precision=hi))
    gb = LANES // SSM_GROUP
    nb = g // gb
    hh, pp = SSM_GROUP, SSM_STATE

    def diag_tiles(a):
        lead, (r, w) = a.shape[1:-2], a.shape[-2:]
        a = jnp.moveaxis(a.reshape((nb, gb) + lead + (r, w)), 1, -3)
        a = jnp.tile(a, (1,) * (a.ndim - 1) + (gb,))
        keep = jnp.arange(gb)[:, None, None] == (jnp.arange(gb * w) // w)[None, None, :]
        return jnp.where(keep, a, 0.0).reshape((nb,) + lead + (gb * r, gb * w))

    rev = (t - 1) - jnp.arange(t)
    kf, kb = jnp.swapaxes(kern[0], -1, -2), jnp.swapaxes(kern[1], -1, -2)
    lag0 = kf[:, 0] + kb[:, 0] + d_skip.reshape(g, hh, 1) * jnp.eye(hh, dtype=F32)
    lags = jnp.concatenate([kb[:, rev[:t - 1]], lag0[:, None], kf[:, 1:]], axis=1)
    tiles = diag_tiles(lags).astype(BF16)
    lag_idx = jnp.arange(t)[None, :] - jnp.arange(t)[:, None] + (t - 1)
    w1y = jnp.transpose(tiles[:, lag_idx], (0, 1, 3, 2, 4)).reshape(nb, t * LANES, t * LANES)
    ends = [jnp.transpose(e, (1, 0, 3, 2)) for e in (lb_r[rev, 0], lb_i[rev, 0], lb_r[:, 1], lb_i[:, 1])]
    w1 = jnp.concatenate([w1y] + [diag_tiles(e).astype(BF16).reshape(nb, t * LANES, gb * pp)
                                  for e in ends], axis=2)
    pf_r, pf_i = pw_r[1:, 0], pw_i[1:, 0]
    pb_r, pb_i = pw_r[rev + 1, 1], pw_i[rev + 1, 1]

    def carry_rows(cr, ci, pr, pi):
        re = cr[None] * pr[:, :, None, :] - ci[None] * pi[:, :, None, :]
        im = cr[None] * pi[:, :, None, :] + ci[None] * pr[:, :, None, :]
        return jnp.transpose(re, (1, 0, 3, 2)), -jnp.transpose(im, (1, 0, 3, 2))

    carries = carry_rows(c_re[0], c_im[0], pf_r, pf_i) + carry_rows(c_re[1], c_im[1], pb_r, pb_i)
    w2 = jnp.concatenate([jnp.transpose(diag_tiles(c).astype(BF16), (0, 2, 1, 3))
                          .reshape(nb, gb * pp, t * LANES) for c in carries], axis=1)
    zeros = jnp.zeros((nb, gb * pp), F32)
    mult = jnp.stack([pw_r[t, 0].reshape(nb, -1), pw_i[t, 0].reshape(nb, -1),
                      pw_r[t, 1].reshape(nb, -1), pw_i[t, 1].reshape(nb, -1),
                      zeros, zeros, zeros, zeros], axis=1)
    return w1.astype(BF16), w2.astype(BF16), mult


def _s5_kernel(u_ref, w1_ref, w2_ref, a_ref, o_ref, r_ref, sp_ref, *, variants):
    t = S5_CHUNK
    nc = u_ref.shape[0] // t
    ny = t * LANES
    ns = sp_ref.shape[1] // 4
    x = jnp.concatenate([u_ref[pl.ds(tau, nc, stride=t), :].astype(BF16) for tau in range(t)], axis=1)
    r_ref[...] = jnp.dot(x, w1_ref[0], preferred_element_type=F32)
    a = a_ref[0]
    afr, afi, abr, abi = a[0:1], a[1:2], a[2:3], a[3:4]

    def scan(nseq, nch):
        zero = jnp.zeros((1, ns), F32)
        state = [(zero, zero, zero, zero)] * nseq
        for c in range(nch):
            for s in range(nseq):
                fr, fi, br, bi = state[s]
                rf = slice(s * nch + c, s * nch + c + 1)
                rb = slice(s * nch + nch - 1 - c, s * nch + nch - c)
                sp_ref[rf, 0:ns] = fr
                sp_ref[rf, ns:2 * ns] = fi
                sp_ref[rb, 2 * ns:3 * ns] = br
                sp_ref[rb, 3 * ns:4 * ns] = bi
                er, ei = r_ref[rf, ny:ny + ns], r_ref[rf, ny + ns:ny + 2 * ns]
                gr, gi = r_ref[rb, ny + 2 * ns:ny + 3 * ns], r_ref[rb, ny + 3 * ns:ny + 4 * ns]
                state[s] = (afr * fr - afi * fi + er, afr * fi + afi * fr + ei,
                            abr * br - abi * bi + gr, abr * bi + abi * br + gi)

    if len(variants) == 1:
        scan(*variants[0][2:])
    else:
        blk = pl.program_id(1)
        for lo, hi, nseq, nch in variants:
            pl.when((blk >= lo) & (blk < hi))(functools.partial(scan, nseq, nch))
    ytot = r_ref[:, 0:ny] + jnp.dot(sp_ref[...].astype(BF16), w2_ref[0], preferred_element_type=F32)
    for tt in range(t):
        o_ref[pl.ds(tt, nc, stride=t), :] = ytot[:, tt * LANES:(tt + 1) * LANES]


def _s5_branch(u, seqs, params):
    n, width = u.shape
    t = S5_CHUNK
    w1, w2, mult = _s5_weights(*params)
    nb = width // LANES
    rblk = max(s for _, s in seqs)
    variants, blk = [], 0
    for bsz, seqlen in seqs:
        assert rblk % seqlen == 0 and (bsz * seqlen) % rblk == 0 and seqlen % t == 0
        nblk = bsz * seqlen // rblk
        variants.append((blk, blk + nblk, rblk // seqlen, seqlen // t))
        blk += nblk
    if all(v[2:] == variants[0][2:] for v in variants):
        variants = [(0, blk) + variants[0][2:]]
    nc = rblk // t
    nst = w1.shape[2] - t * LANES
    return pl.pallas_call(
        functools.partial(_s5_kernel, variants=tuple(variants)),
        grid=(nb, n // rblk),
        in_specs=[
            pl.BlockSpec((rblk, LANES), lambda b, r: (r, b)),
            pl.BlockSpec((1,) + w1.shape[1:], lambda b, r: (b, 0, 0), pipeline_mode=pl.Buffered(1)),
            pl.BlockSpec((1,) + w2.shape[1:], lambda b, r: (b, 0, 0), pipeline_mode=pl.Buffered(1)),
            pl.BlockSpec((1,) + mult.shape[1:], lambda b, r: (b, 0, 0)),
        ],
        out_specs=pl.BlockSpec((rblk, LANES), lambda b, r: (r, b)),
        out_shape=jax.ShapeDtypeStruct((n, width), F32),
        scratch_shapes=[pltpu.VMEM((nc, w1.shape[2]), F32), pltpu.VMEM((nc, nst), F32)],
        compiler_params=_cparams("parallel", "parallel"),
        name="s5_chunks",
    )(u, w1, w2, mult)


def _conv_kernel(vp_ref, gp_ref, vc_ref, gc_ref, vn_ref, gn_ref, w_ref, cb_ref, lg_ref, lb_ref,
                 o_ref, ext_ref, acc_ref, *, tl, bounds):
    r0 = pl.program_id(0) * tl
    pos, length = r0, bounds[0][1]
    for start, seqlen in bounds:
        inside = r0 >= start
        pos = jnp.where(inside, lax.rem(r0 - start, seqlen), pos)
        length = jnp.where(inside, seqlen, length)
    left_ok = (pos != 0).astype(F32)
    right_ok = (pos + tl != length).astype(F32)

    def glu(v_ref, g_ref):
        return v_ref[...].astype(F32) * _sigmoid(g_ref[...].astype(F32))

    ext_ref[0:HALO, :] = glu(vp_ref, gp_ref) * left_ok
    ext_ref[HALO:HALO + tl, :] = glu(vc_ref, gc_ref)
    ext_ref[HALO + tl:2 * HALO + tl, :] = glu(vn_ref, gn_ref) * right_ok

    cdim = o_ref.shape[1]
    rb = 64
    off = HALO - CONV_PAD

    def rows(i, carry):
        base = pl.multiple_of(i * rb, rb)
        for lb in range(cdim // LANES):
            lanes = slice(lb * LANES, (lb + 1) * LANES)
            win = ext_ref[pl.ds(base, rb + 2 * HALO), lanes]
            acc = jnp.zeros((rb, LANES), F32) + cb_ref[:, lanes]
            for k in range(CONV_WIDTH):
                acc = acc + w_ref[k:k + 1, lanes] * win[off + k:off + k + rb, :]
            acc_ref[pl.ds(base, rb), lanes] = acc
        return carry

    lax.fori_loop(0, tl // rb, rows, 0)
    hf = acc_ref[...]
    mu = jnp.mean(hf, axis=-1, keepdims=True)
    cen = hf - mu
    var = jnp.mean(cen * cen, axis=-1, keepdims=True)
    hn = cen * lax.rsqrt(var + EPS) * lg_ref[...] + lb_ref[...]
    o_ref[...] = (hn * _sigmoid(hn)).astype(o_ref.dtype)


def _conv_branch(z, col0, cdim, seqs, conv_w, conv_b, ln_g, ln_b):
    n = z.shape[0]
    tl = _tile(min(s for _, s in seqs), 256)
    hb = tl // HALO
    nhb = n // HALO
    bounds, tok = [], 0
    for bsz, seqlen in seqs:
        bounds.append((tok, seqlen))
        tok += bsz * seqlen
    cur = lambda c: pl.BlockSpec((tl, cdim), lambda i: (i, c))
    prev = lambda c: pl.BlockSpec((HALO, cdim), lambda i: (jnp.maximum(i * hb - 1, 0), c))
    nxt = lambda c: pl.BlockSpec((HALO, cdim), lambda i: (jnp.minimum((i + 1) * hb, nhb - 1), c))
    vec = pl.BlockSpec((1, cdim), lambda i: (0, 0))
    return pl.pallas_call(
        functools.partial(_conv_kernel, tl=tl, bounds=tuple(bounds)),
        grid=(n // tl,),
        in_specs=[prev(col0), prev(col0 + 1), cur(col0), cur(col0 + 1), nxt(col0), nxt(col0 + 1),
                  pl.BlockSpec((CONV_WIDTH, cdim), lambda i: (0, 0)), vec, vec, vec],
        out_specs=pl.BlockSpec((tl, cdim), lambda i: (i, 0)),
        out_shape=jax.ShapeDtypeStruct((n, cdim), BF16),
        scratch_shapes=[pltpu.VMEM((tl + 2 * HALO, cdim), F32), pltpu.VMEM((tl, cdim), F32)],
        compiler_params=_cparams("parallel"),
        name="conv_branch",
    )(z, z, z, z, z, z, conv_w, conv_b.reshape(1, cdim), ln_g.reshape(1, cdim), ln_b.reshape(1, cdim))


def _mixer_out_kernel(y_ref, cb_ref, ga_ref, gb_ref, x_ref, wglu_ref, bglu_ref, wa_ref, wb_ref,
                      wo_ref, o_ref):
    hg = _gelu(y_ref[...])
    gate = _sigmoid(jnp.dot(hg.astype(BF16), wglu_ref[...], preferred_element_type=F32)
                    + bglu_ref[...])
    a = jnp.dot((hg * gate).astype(BF16), wa_ref[...], preferred_element_type=F32)
    b = jnp.dot(cb_ref[...], wb_ref[...], preferred_element_type=F32)
    merged = ga_ref[...].astype(F32) * a + gb_ref[...].astype(F32) * b
    o_ref[...] = x_ref[...] + jnp.dot(merged.astype(BF16), wo_ref[...], preferred_element_type=F32)


def _mixer_out(y, cb, sg, x, w_glu, b_glu, w_a, w_b, w_o):
    n, d = x.shape
    sw = y.shape[1]
    cd = cb.shape[1]
    tm = _tile(n, 256)
    return pl.pallas_call(
        _mixer_out_kernel,
        grid=(n // tm,),
        in_specs=[
            pl.BlockSpec((tm, sw), lambda i: (i, 0)),
            pl.BlockSpec((tm, cd), lambda i: (i, 0)),
            pl.BlockSpec((tm, d), lambda i: (i, 0)),
            pl.BlockSpec((tm, d), lambda i: (i, 1)),
            pl.BlockSpec((tm, d), lambda i: (i, 0)),
            _resident(w_glu.shape), _resident((1, sw)), _resident(w_a.shape), _resident(w_b.shape),
            _resident(w_o.shape),
        ],
        out_specs=pl.BlockSpec((tm, d), lambda i: (i, 0)),
        out_shape=jax.ShapeDtypeStruct((n, d), F32),
        compiler_params=_cparams("parallel"),
        name="mixer_out",
    )(y, cb, sg, sg, x, w_glu, b_glu.reshape(1, sw), w_a, w_b, w_o)


def _top_rows(s, iota, count):
    nrows = s.shape[0]
    vals, idxs = [], []
    for _ in range(count):
        m = jnp.max(s, axis=0, keepdims=True)
        idx = jnp.min(jnp.where(s == m, iota, float(nrows)), axis=0, keepdims=True)
        vals.append(m)
        idxs.append(idx)
        s = jnp.where(iota == idx, -jnp.inf, s)
    return vals, idxs


def _peer_route_kernel(x_ref, g_ref, wq_ref, keys_ref, xn_ref, idx_ref, gate_ref, it_ref, gt_ref,
                       cs_ref, ce_ref):
    tm = x_ref.shape[0]
    xn = _rms(x_ref[...], g_ref[...])
    xn_ref[...] = xn
    q = jnp.dot(xn.astype(BF16), wq_ref[...], preferred_element_type=F32).astype(BF16)
    iota_k = lax.broadcasted_iota(I32, (N_KEYS, tm), 0).astype(F32)
    pairs = CAND_PAIRS
    ncand = N_CAND
    iota_c = lax.broadcasted_iota(I32, (ncand, tm), 0).astype(F32)
    for h in range(PEER_HEADS):
        tops = []
        for c in range(2):
            qc = q[:, (2 * h + c) * PEER_HALF:(2 * h + c + 1) * PEER_HALF]
            s = lax.dot_general(keys_ref[h, c], qc, (((1,), (1,)), ((), ())),
                                preferred_element_type=F32)
            tops.append(_top_rows(s, iota_k, PEER_TOPK))
        (s1, i1), (s2, i2) = tops
        cs_ref[...] = jnp.full((ncand, tm), -jnp.inf, F32)
        ce_ref[...] = jnp.zeros((ncand, tm), F32)
        for r, (i, j) in enumerate(pairs):
            cs_ref[r:r + 1, :] = s1[i] + s2[j]
            ce_ref[r:r + 1, :] = i1[i] * float(N_KEYS) + i2[j]
        cs = cs_ref[...]
        ce = ce_ref[...]
        best = []
        for k in range(PEER_TOPK):
            m = jnp.max(cs, axis=0, keepdims=True)
            pos = jnp.min(jnp.where(cs == m, iota_c, float(ncand)), axis=0, keepdims=True)
            hit = iota_c == pos
            it_ref[h * PEER_TOPK + k:h * PEER_TOPK + k + 1, :] = jnp.max(
                jnp.where(hit, ce, -1.0), axis=0, keepdims=True)
            best.append(m)
            cs = jnp.where(hit, -jnp.inf, cs)
        ex = [jnp.exp(b - best[0]) for b in best]
        inv = 1.0 / functools.reduce(lambda p, r: p + r, ex)
        for k in range(PEER_TOPK):
            gt_ref[h * PEER_TOPK + k:h * PEER_TOPK + k + 1, :] = ex[k] * inv
    idx_ref[...] = it_ref[...].T.astype(I32)
    gate_ref[...] = gt_ref[...].T


def _peer_route(x, g, w_q, keys):
    n, d = x.shape
    tm = _tile(n, 256)
    qcols = w_q.shape[1]
    return pl.pallas_call(
        _peer_route_kernel,
        grid=(n // tm,),
        in_specs=[
            pl.BlockSpec((tm, d), lambda i: (i, 0)),
            _resident((1, d)), _resident((d, qcols)), _resident(keys.shape),
        ],
        out_specs=[
            pl.BlockSpec((tm, d), lambda i: (i, 0)),
            pl.BlockSpec((tm, HK), lambda i: (i, 0)),
            pl.BlockSpec((tm, HK), lambda i: (i, 0)),
        ],
        out_shape=[
            jax.ShapeDtypeStruct((n, d), F32),
            jax.ShapeDtypeStruct((n, HK), I32),
            jax.ShapeDtypeStruct((n, HK), F32),
        ],
        scratch_shapes=[pltpu.VMEM((HK, tm), F32), pltpu.VMEM((HK, tm), F32),
                        pltpu.VMEM((N_CAND, tm), F32), pltpu.VMEM((N_CAND, tm), F32)],
        compiler_params=_cparams("parallel"),
        name="peer_route",
    )(x, g.reshape(1, d), w_q, keys)


EXPERT_SLOTS = 8
LOOKAHEAD = EXPERT_SLOTS - 2
TOKEN_GROUP = 16


def _pack_tables_kernel(u_ref, v_ref, o_ref):
    packed = pltpu.pack_elementwise([v_ref[...], u_ref[...]], packed_dtype=BF16)
    for r in range(packed.shape[0]):
        o_ref[r] = packed[r:r + 1, :]


def _pack_expert_tables(u_tab, v_tab):
    e, d = u_tab.shape
    te = _tile(e, 256)
    return pl.pallas_call(
        _pack_tables_kernel,
        grid=(e // te,),
        in_specs=[pl.BlockSpec((te, d), lambda i: (i, 0)), pl.BlockSpec((te, d), lambda i: (i, 0))],
        out_specs=pl.BlockSpec((te, 1, d), lambda i: (i, 0, 0)),
        out_shape=jax.ShapeDtypeStruct((e, 1, d), jnp.uint32),
        compiler_params=_cparams("parallel"),
        name="pack_tables",
    )(u_tab, v_tab)


def _peer_experts_kernel(idx_ref, idx_next_ref, gate_ref, xn_ref, x_ref, tab_ref, o_ref, *scratch):
    bufs = scratch[:EXPERT_SLOTS]
    sem_ref, xs_ref, os_ref = scratch[EXPERT_SLOTS:]
    tb, d = xn_ref.shape
    nlb = d // LANES
    ngroups = tb // TOKEN_GROUP
    step = pl.program_id(0)

    def issuer(ref, row, slot):
        ks = iter(range(HK))

        def emit(n):
            for k in [k for _, k in zip(range(n), ks)]:
                pltpu.make_async_copy(tab_ref.at[ref[row, k]],
                                      bufs[slot].at[pl.ds(k, 1), :], sem_ref.at[slot]
                                      ).start(priority=k % 2)
        return emit

    def issue(ref, row, slot):
        issuer(ref, row, slot)(HK)

    def wait(slot):
        pltpu.make_async_copy(bufs[slot], bufs[slot], sem_ref.at[slot]).wait()

    @pl.when(step == 0)
    def _():
        for t in range(LOOKAHEAD):
            issue(idx_ref, t, t)

    def first_half(i, slot, gate, emit, per):
        acc = jnp.zeros((HK, LANES), F32)
        for j in range(nlb):
            lanes = slice(j * LANES, (j + 1) * LANES)
            u = pltpu.unpack_elementwise(bufs[slot][:, lanes], index=1, packed_dtype=BF16,
                                         unpacked_dtype=F32)
            acc = acc + u * xs_ref[i:i + 1, lanes]
            emit(per)
        s = jnp.sum(acc.T, axis=0, keepdims=True)
        act = _gelu(s) * gate
        return jnp.broadcast_to(act, (LANES, HK)).T

    def second_half(i, slot, act_col, emit, per):
        for j in range(nlb):
            lanes = slice(j * LANES, (j + 1) * LANES)
            v = pltpu.unpack_elementwise(bufs[slot][:, lanes], index=0, packed_dtype=BF16,
                                         unpacked_dtype=F32)
            os_ref[i:i + 1, lanes] += jnp.sum(v * act_col, axis=0, keepdims=True)
            emit(per)

    def group(t0, last):
        rows = pl.ds(t0, TOKEN_GROUP)
        xs_ref[...] = xn_ref[rows, :]
        os_ref[...] = x_ref[rows, :]
        gates = gate_ref[rows, :]
        act_col = None
        for i in range(TOKEN_GROUP):
            slot = i % EXPERT_SLOTS
            wait(slot)
            ahead = i + LOOKAHEAD
            if last and ahead >= TOKEN_GROUP:
                emit = issuer(idx_next_ref, ahead - TOKEN_GROUP, ahead % EXPERT_SLOTS)
            else:
                emit = issuer(idx_ref, t0 + ahead, ahead % EXPERT_SLOTS)
            per = -(-HK // (nlb * (2 if i > 0 else 1)))
            nxt = first_half(i, slot, gates[i:i + 1, :], emit, per)
            if i > 0:
                second_half(i - 1, (i - 1) % EXPERT_SLOTS, act_col, emit, per)
            emit(HK)
            act_col = nxt
        second_half(TOKEN_GROUP - 1, (TOKEN_GROUP - 1) % EXPERT_SLOTS, act_col, lambda n: None, 0)
        o_ref[rows, :] = os_ref[...]

    def body(g, carry):
        group(pl.multiple_of(g * TOKEN_GROUP, TOKEN_GROUP), False)
        return carry

    lax.fori_loop(0, ngroups - 1, body, 0)
    group((ngroups - 1) * TOKEN_GROUP, True)

    @pl.when(step == pl.num_programs(0) - 1)
    def _():
        for t in range(LOOKAHEAD):
            wait(t)


def _peer_experts(idx, gates, xn, x, table):
    n, d = x.shape
    tb = _tile(n, 128)
    assert tb % TOKEN_GROUP == 0 and TOKEN_GROUP % EXPERT_SLOTS == 0 and LOOKAHEAD <= 8
    nrb = n // 8
    next_rows = pl.BlockSpec((8, HK), lambda i: (jnp.minimum((i + 1) * (tb // 8), nrb - 1), 0),
                             memory_space=pltpu.SMEM)
    return pl.pallas_call(
        _peer_experts_kernel,
        grid=(n // tb,),
        in_specs=[
            pl.BlockSpec((tb, HK), lambda i: (i, 0), memory_space=pltpu.SMEM),
            next_rows,
            pl.BlockSpec((tb, HK), lambda i: (i, 0)),
            pl.BlockSpec((tb, d), lambda i: (i, 0)),
            pl.BlockSpec((tb, d), lambda i: (i, 0)),
            pl.BlockSpec(memory_space=pl.ANY),
        ],
        out_specs=pl.BlockSpec((tb, d), lambda i: (i, 0)),
        out_shape=jax.ShapeDtypeStruct((n, d), F32),
        scratch_shapes=[pltpu.VMEM((HK, d), jnp.uint32) for _ in range(EXPERT_SLOTS)]
                       + [pltpu.SemaphoreType.DMA((EXPERT_SLOTS,)),
                          pltpu.VMEM((TOKEN_GROUP, d), F32), pltpu.VMEM((TOKEN_GROUP, d), F32)],
        compiler_params=_cparams("arbitrary"),
        name="peer_experts",
    )(idx, idx, gates, xn, x, table)


def _ple_final_kernel(x_ref, p_ref, gp_ref, wg_ref, bg_ref, wp_ref, gf_ref, *o_refs, final, split):
    x = x_ref[...]
    pg = _sigmoid(jnp.dot(_rms(x, gp_ref[...]).astype(BF16), wg_ref[...],
                          preferred_element_type=F32) + bg_ref[...])
    pe = jnp.dot(p_ref[...].astype(BF16), wp_ref[...], preferred_element_type=F32)
    y = x + pe * pg
    y = _rms(y, gf_ref[...]) if final else y
    if split is None:
        o_refs[0][...] = y
    else:
        first = pl.program_id(0) < split

        @pl.when(first)
        def _():
            o_refs[0][...] = y

        @pl.when(jnp.logical_not(first))
        def _():
            o_refs[1][...] = y


def _ple_final(x, p, g_ple, w_gate, b_gate, w_ple, g_final, final, n_first=None):
    n, d = x.shape
    pd = p.shape[1]
    tm = _tile(n, 256)
    if n_first is None:
        split = None
        out_specs = pl.BlockSpec((tm, d), lambda i: (i, 0))
        out_shape = jax.ShapeDtypeStruct((n, d), F32)
    else:
        assert n_first % tm == 0 and 0 < n_first < n
        split = n_first // tm
        out_specs = [pl.BlockSpec((tm, d), lambda i: (jnp.minimum(i, split - 1), 0)),
                     pl.BlockSpec((tm, d), lambda i: (jnp.maximum(i - split, 0), 0))]
        out_shape = [jax.ShapeDtypeStruct((n_first, d), F32), jax.ShapeDtypeStruct((n - n_first, d), F32)]
    return pl.pallas_call(
        functools.partial(_ple_final_kernel, final=final, split=split),
        grid=(n // tm,),
        in_specs=[
            pl.BlockSpec((tm, d), lambda i: (i, 0)),
            pl.BlockSpec((tm, pd), lambda i: (i, 0)),
            _resident((1, d)), _resident(w_gate.shape), _resident((1, d)), _resident(w_ple.shape),
            _resident((1, d)),
        ],
        out_specs=out_specs,
        out_shape=out_shape,
        compiler_params=_cparams("arbitrary"),
        name="ple_final",
    )(x, p, g_ple.reshape(1, d), w_gate, b_gate.reshape(1, d), w_ple, g_final.reshape(1, d))


def kernel(x_prompt, x_sample, p_prompt, p_sample, norm_mix_g, w_in, w_gate, b_gate, ssm_lam_re, ssm_lam_im, ssm_b_re, ssm_b_im, ssm_c_re, ssm_c_im, ssm_log_step, ssm_d, ssm_w_glu, ssm_b_glu, ssm_w_out, conv_w, conv_b, conv_ln_g, conv_ln_b, conv_w_out, w_o, norm_ffn_g, peer_w_q, peer_keys, peer_u, peer_v, norm_ple_g, ple_w, ple_w_gate, ple_b_gate, norm_final_g):
    depth = w_in.shape[0]
    d = x_prompt.shape[-1]
    seqs = (x_prompt.shape[:2], x_sample.shape[:2])
    n_p = seqs[0][0] * seqs[0][1]
    x = jnp.concatenate([x_prompt.reshape(-1, d), x_sample.reshape(-1, d)], axis=0)
    sw = ssm_w_glu.shape[1]
    cdim = conv_w.shape[2]
    assert sw == cdim and w_in.shape[2] == sw + 2 * cdim and sw % (2 * SSM_GROUP) == 0
    for i in range(depth):
        p = jnp.concatenate([p_prompt[i].reshape(n_p, -1), p_sample[i].reshape(x.shape[0] - n_p, -1)], axis=0)
        w_in_b = w_in[i].astype(BF16)
        zs = _norm_matmul(x, norm_mix_g[i], w_in_b[:, :sw], jnp.zeros((sw,), F32), "none", F32)
        zc = _norm_matmul(x, norm_mix_g[i], w_in_b[:, sw:], jnp.zeros((2 * cdim,), F32), "none", BF16)
        sg = _norm_matmul(x, norm_mix_g[i], w_gate[i].astype(BF16), b_gate[i], "sigmoid", BF16)
        y = _s5_branch(zs, seqs,
                       (ssm_lam_re[i], ssm_lam_im[i], ssm_b_re[i], ssm_b_im[i], ssm_c_re[i],
                        ssm_c_im[i], ssm_log_step[i], ssm_d[i]))
        cb = _conv_branch(zc, 0, cdim, seqs, conv_w[i], conv_b[i], conv_ln_g[i], conv_ln_b[i])
        x = _mixer_out(y, cb, sg, x, ssm_w_glu[i].astype(BF16), ssm_b_glu[i], ssm_w_out[i].astype(BF16),
                       conv_w_out[i].astype(BF16), w_o[i].astype(BF16))
        xn, idx, gates = _peer_route(x, norm_ffn_g[i], peer_w_q[i].astype(BF16), peer_keys[i].astype(BF16))
        x = _peer_experts(idx, gates, xn, x, _pack_expert_tables(peer_u[i], peer_v[i]))
        x = _ple_final(x, p, norm_ple_g[i], ple_w_gate[i].astype(BF16), ple_b_gate[i],
                       ple_w[i].astype(BF16), norm_final_g, i == depth - 1,
                       n_first=n_p if i == depth - 1 else None)
    y_prompt, y_sample = x
    return (y_prompt.reshape(x_prompt.shape), y_sample.reshape(x_sample.shape))
```
